```python
import jax, jax.numpy as jnp
from jax import lax
import numpy as np

D_MODEL = 1024
BATCH = 8
SEQ = 4096
DEPTH = 1
DEC_BATCH = 8
DEC_SEQ = 32
PAST_LEN = 4096

CHUNK = 64
SG_CHUNK = 128
D_SG = D_MODEL
SG_GROUPS = 8
SG_GROUP_DIM = D_SG // SG_GROUPS
D_CONV = D_MODEL
CONV_WIDTH = 31
D_FF = 4 * D_MODEL
PLE_DIM = 256
IN_COLS = 2 * D_SG + 2 * D_CONV + 2 * D_MODEL
EPS = 1e-6

kernel_name = "gmlp_conformer_conv_gated_hybrid_step"


def _rmsnorm(x, g):
    xf = x.astype(jnp.float32)
    y = xf * lax.rsqrt(jnp.mean(xf * xf, axis=-1, keepdims=True) + EPS)
    return (y * g.astype(jnp.float32)).astype(x.dtype)


def _layernorm(x, g, b):
    xf = x.astype(jnp.float32)
    mu = jnp.mean(xf, axis=-1, keepdims=True)
    xc = xf - mu
    y = xc * lax.rsqrt(jnp.mean(xc * xc, axis=-1, keepdims=True) + EPS)
    return (y * g.astype(jnp.float32) + b.astype(jnp.float32)).astype(x.dtype)


def _layer(x, p, conv_hist, g_mix, w_in, g_sgv, b_sgv, w_sg, b_sg, w_a, w_dw, b_dw,
           g_cln, b_cln, w_b, w_out, g_ffn, w_up, w_down, g_ple, w_ple_gate, w_ple):
    B, T, _ = x.shape
    n = _rmsnorm(x, g_mix)
    z = n @ w_in
    o1 = D_SG
    o2 = 2 * D_SG
    o3 = o2 + D_CONV
    o4 = o3 + D_CONV
    o5 = o4 + D_MODEL
    u, v, ca, cb, ga, gb = jnp.split(z, [o1, o2, o3, o4, o5], axis=-1)

    vn = _layernorm(v, g_sgv, b_sgv)
    L = min(T, SG_CHUNK)
    vc = vn.reshape(B, T // L, L, SG_GROUPS, SG_GROUP_DIM)
    wm = jnp.tril(w_sg[:, :L, :L])
    s = jnp.einsum('gts,bnsgc->bntgc', wm, vc)
    s = s + jnp.transpose(b_sg[:, :L])[None, None, :, :, None]
    a = u * s.reshape(B, T, D_SG)

    c = ca * jax.nn.sigmoid(cb)
    cfull = jnp.concatenate([conv_hist.astype(c.dtype), c], axis=1)
    dw = lax.conv_general_dilated(
        cfull, w_dw[:, None, :].astype(cfull.dtype), window_strides=(1,), padding='VALID',
        dimension_numbers=('NWC', 'WIO', 'NWC'), feature_group_count=D_CONV) + b_dw
    new_hist = cfull[:, -(CONV_WIDTH - 1):]
    bb = jax.nn.silu(_layernorm(dw, g_cln, b_cln))

    mix = jax.nn.sigmoid(ga) * (a @ w_a) + jax.nn.sigmoid(gb) * (bb @ w_b)
    h = x + mix @ w_out

    f = _rmsnorm(h, g_ffn) @ w_up
    h = h + jnp.square(jax.nn.relu(f)) @ w_down

    gate = jax.nn.sigmoid(_rmsnorm(h, g_ple) @ w_ple_gate)
    h = h + gate * (p @ w_ple)
    return h, new_hist, vn


def setup_inputs(seed: int = 0) -> dict:
    key = jax.random.key(seed)
    ks = jax.random.split(key, 26)
    f32 = jnp.float32

    def nrm(k, shape, scale):
        return jax.random.normal(k, shape, f32) * scale

    def gain(k, shape):
        return 1.0 + 0.02 * jax.random.normal(k, shape, f32)

    return {
        "x_prompt": nrm(ks[0], (BATCH, SEQ, D_MODEL), 1.0),
        "x_sample": nrm(ks[1], (DEC_BATCH, DEC_SEQ, D_MODEL), 1.0),
        "cache_conv": nrm(ks[2], (DEPTH, DEC_BATCH, CONV_WIDTH - 1, D_CONV), 0.5),
        "p_prompt": nrm(ks[3], (DEPTH, BATCH, SEQ, PLE_DIM), 1.0),
        "p_sample": nrm(ks[4], (DEPTH, DEC_BATCH, DEC_SEQ, PLE_DIM), 1.0),
        "g_mix": gain(ks[5], (DEPTH, D_MODEL)),
        "w_in": nrm(ks[6], (DEPTH, D_MODEL, IN_COLS), D_MODEL ** -0.5),
        "g_sgv": gain(ks[7], (DEPTH, D_SG)),
        "b_sgv": nrm(ks[8], (DEPTH, D_SG), 0.02),
        "w_sg": nrm(ks[9], (DEPTH, SG_GROUPS, SG_CHUNK, SG_CHUNK), SG_CHUNK ** -0.5),
        "b_sg": 1.0 + nrm(ks[10], (DEPTH, SG_GROUPS, SG_CHUNK), 0.1),
        "w_a": nrm(ks[11], (DEPTH, D_SG, D_MODEL), D_SG ** -0.5),
        "w_dw": nrm(ks[12], (DEPTH, CONV_WIDTH, D_CONV), CONV_WIDTH ** -0.5),
        "b_dw": nrm(ks[13], (DEPTH, D_CONV), 0.02),
        "g_cln": gain(ks[14], (DEPTH, D_CONV)),
        "b_cln": nrm(ks[15], (DEPTH, D_CONV), 0.02),
        "w_b": nrm(ks[16], (DEPTH, D_CONV, D_MODEL), D_CONV ** -0.5),
        "w_out": nrm(ks[17], (DEPTH, D_MODEL, D_MODEL), D_MODEL ** -0.5),
        "g_ffn": gain(ks[18], (DEPTH, D_MODEL)),
        "w_up": nrm(ks[19], (DEPTH, D_MODEL, D_FF), D_MODEL ** -0.5),
        "w_down": nrm(ks[20], (DEPTH, D_FF, D_MODEL), D_FF ** -0.5),
        "g_ple": gain(ks[21], (DEPTH, D_MODEL)),
        "w_ple_gate": nrm(ks[22], (DEPTH, D_MODEL, D_MODEL), D_MODEL ** -0.5),
        "w_ple": nrm(ks[23], (DEPTH, PLE_DIM, D_MODEL), PLE_DIM ** -0.5),
        "g_final": gain(ks[24], (D_MODEL,)),
    }


def reference(x_prompt, x_sample, cache_conv, p_prompt, p_sample, g_mix, w_in, g_sgv, b_sgv,
              w_sg, b_sg, w_a, w_dw, b_dw, g_cln, b_cln, w_b, w_out, g_ffn, w_up, w_down,
              g_ple, w_ple_gate, w_ple, g_final):
    hp = x_prompt
    hs = x_sample
    conv_p, conv_s, v_s = [], [], []
    for i in range(DEPTH):
        w = (g_mix[i], w_in[i], g_sgv[i], b_sgv[i], w_sg[i], b_sg[i], w_a[i], w_dw[i], b_dw[i],
             g_cln[i], b_cln[i], w_b[i], w_out[i], g_ffn[i], w_up[i], w_down[i],
             g_ple[i], w_ple_gate[i], w_ple[i])
        zero_hist = jnp.zeros((hp.shape[0], CONV_WIDTH - 1, D_CONV), hp.dtype)
        hp, cp, _ = _layer(hp, p_prompt[i], zero_hist, *w)
        hs, cs, vs = _layer(hs, p_sample[i], cache_conv[i], *w)
        conv_p.append(cp)
        conv_s.append(cs)
        v_s.append(vs)
    y_prompt = _rmsnorm(hp, g_final)
    y_sample = _rmsnorm(hs, g_final)
    conv_state_prompt = jnp.stack(conv_p)
    conv_state_sample = jnp.stack(conv_s)
    sg_v_sample = jnp.stack(v_s)
    return (y_prompt, y_sample, conv_state_prompt, conv_state_sample, sg_v_sample)
```

```python
import functools

import jax
import jax.numpy as jnp
from jax import lax
from jax.experimental import pallas as pl
from jax.experimental.pallas import tpu as pltpu

D_MODEL = 1024
SG_CHUNK = 128
SG_GROUPS = 8
SG_GROUP_DIM = D_MODEL // SG_GROUPS
CONV_WIDTH = 31
HIST = CONV_WIDTH - 1
D_FF = 4 * D_MODEL
PLE_DIM = 256
EPS = 1e-6

LANES = 128
SUBLANES = 8
COL_BLOCKS = D_MODEL // LANES
HIST_PAD = 32
HIST_OFF = HIST_PAD - HIST
CONV_ROW_BLOCK = 64

MIXER_ROWS = 256
FFN_ROWS = 512
VMEM_LIMIT_BYTES = 56 * 1024 * 1024

BF16 = jnp.bfloat16
F32 = jnp.float32


def _rms(x, g):
    ms = jnp.mean(x * x, axis=-1, keepdims=True)
    return x * lax.rsqrt(ms + EPS) * g


def _ln(x, g, b):
    mu = jnp.mean(x, axis=-1, keepdims=True)
    xc = x - mu
    y = xc * lax.rsqrt(jnp.mean(xc * xc, axis=-1, keepdims=True) + EPS)
    return y * g + b


def _dot(a, b):
    return jnp.dot(a, b, preferred_element_type=F32)


def _depthwise_conv(cbuf, seq, rows, wdw_ref):
    cols = []
    for j in range(COL_BLOCKS):
        lanes = slice(j * LANES, (j + 1) * LANES)
        blocks = []
        for r0 in range(0, rows, CONV_ROW_BLOCK):
            rb = min(CONV_ROW_BLOCK, rows - r0)
            acc = None
            for k in range(CONV_WIDTH):
                w = wdw_ref[k:k + 1, lanes]
                term = cbuf[seq * COL_BLOCKS + j, pl.ds(HIST_OFF + k + r0, rb), :] * w
                acc = term if acc is None else acc + term
            blocks.append(acc)
        cols.append(jnp.concatenate(blocks, axis=0) if len(blocks) > 1 else blocks[0])
    return jnp.concatenate(cols, axis=1)


def _mixer_body(x_ref, hist_ref, gmix, win, gsgv, bsgv, wsg, bsg, wa, wdw, bdw, gcln, bcln,
                wb, wout, h_ref, cs_ref, vn_ref, cbuf, *, nseq, rows, chunk, step, nsteps):
    d = D_MODEL
    total = nseq * rows
    x = x_ref[...]
    n = _rms(x, gmix[...]).astype(BF16)

    v = _dot(n, win[:, d:2 * d])
    vn = _ln(v, gsgv[...], bsgv[...])
    if vn_ref is not None:
        vn_ref[...] = vn
    vnb = vn.astype(BF16)
    nch = total // chunk
    per_group = []
    for g in range(SG_GROUPS):
        lanes = slice(g * SG_GROUP_DIM, (g + 1) * SG_GROUP_DIM)
        vg = jnp.concatenate([vnb[c * chunk:(c + 1) * chunk, lanes] for c in range(nch)], axis=1)
        per_group.append(_dot(wsg[g], vg))
    bias = bsg[0:chunk, :]
    s_rows = []
    for c in range(nch):
        lanes = slice(c * SG_GROUP_DIM, (c + 1) * SG_GROUP_DIM)
        s_rows.append(jnp.concatenate([sg[:, lanes] for sg in per_group], axis=1) + bias)
    s = jnp.concatenate(s_rows, axis=0)
    u = _dot(n, win[:, 0:d])
    a_mix = _dot((u * s).astype(BF16), wa[...])

    c = _dot(n, win[:, 2 * d:3 * d]) * jax.nn.sigmoid(_dot(n, win[:, 3 * d:4 * d]))
    if hist_ref is None:
        @pl.when(step == 0)
        def _():
            cbuf[:, 0:HIST_PAD, :] = jnp.zeros((nseq * COL_BLOCKS, HIST_PAD, LANES), F32)
    for q in range(nseq):
        for j in range(COL_BLOCKS):
            lanes = slice(j * LANES, (j + 1) * LANES)
            if hist_ref is not None:
                cbuf[q * COL_BLOCKS + j, HIST_OFF:HIST_PAD, :] = hist_ref[q, :, lanes]
            cbuf[q * COL_BLOCKS + j, HIST_PAD:HIST_PAD + rows, :] = c[q * rows:(q + 1) * rows, lanes]
    dws = [_depthwise_conv(cbuf, q, rows, wdw) for q in range(nseq)]
    dw = (jnp.concatenate(dws, axis=0) if nseq > 1 else dws[0]) + bdw[...]

    def write_state():
        for q in range(nseq):
            for j in range(COL_BLOCKS):
                lanes = slice(j * LANES, (j + 1) * LANES)
                tail = cbuf[q * COL_BLOCKS + j, rows + HIST_OFF:rows + HIST_PAD, :]
                if nseq > 1:
                    cs_ref[q, :, lanes] = tail
                else:
                    cs_ref[:, lanes] = tail

    if hist_ref is None:
        pl.when(step == nsteps - 1)(write_state)
        cbuf[:, 0:HIST_PAD, :] = cbuf[:, rows:rows + HIST_PAD, :]
    else:
        write_state()

    bb = jax.nn.silu(_ln(dw, gcln[...], bcln[...])).astype(BF16)
    b_mix = _dot(bb, wb[...])

    ga = _dot(n, win[:, 4 * d:5 * d])
    gb = _dot(n, win[:, 5 * d:6 * d])
    mix = (jax.nn.sigmoid(ga) * a_mix + jax.nn.sigmoid(gb) * b_mix).astype(BF16)
    h_ref[...] = x + _dot(mix, wout[...])


def _mixer_prompt_kernel(x_ref, gmix, win, gsgv, bsgv, wsg, bsg, wa, wdw, bdw, gcln, bcln,
                         wb, wout, h_ref, cs_ref, cbuf, *, rows, chunk):
    _mixer_body(x_ref, None, gmix, win, gsgv, bsgv, wsg, bsg, wa, wdw, bdw, gcln, bcln,
                wb, wout, h_ref, cs_ref, None, cbuf, nseq=1, rows=rows, chunk=chunk,
                step=pl.program_id(1), nsteps=pl.num_programs(1))


def _mixer_sample_kernel(x_ref, hist_ref, gmix, win, gsgv, bsgv, wsg, bsg, wa, wdw, bdw, gcln,
                         bcln, wb, wout, h_ref, cs_ref, vn_ref, cbuf, *, nseq, rows, chunk):
    _mixer_body(x_ref, hist_ref, gmix, win, gsgv, bsgv, wsg, bsg, wa, wdw, bdw, gcln, bcln,
                wb, wout, h_ref, cs_ref, vn_ref, cbuf, nseq=nseq, rows=rows, chunk=chunk,
                step=None, nsteps=None)


def _ffn_kernel(h_ref, p_ref, gffn, wup, wdown, gple, wgate, wple, gfinal, y_ref):
    h = h_ref[...]
    f = _dot(_rms(h, gffn[...]).astype(BF16), wup[...])
    r = jnp.square(jnp.maximum(f, 0.0)).astype(BF16)
    h = h + _dot(r, wdown[...])
    gate = jax.nn.sigmoid(_dot(_rms(h, gple[...]).astype(BF16), wgate[...]))
    e = _dot(p_ref[...].astype(BF16), wple[...])
    h = h + gate * e
    y_ref[...] = _rms(h, gfinal[...])


def _resident(shape):
    zeros = (0,) * len(shape)
    return pl.BlockSpec(shape, lambda *_: zeros, pipeline_mode=pl.Buffered(1))


def _full(shape):
    zeros = (0,) * len(shape)
    return pl.BlockSpec(shape, lambda *_: zeros)


def _params(semantics):
    return pltpu.CompilerParams(dimension_semantics=semantics, vmem_limit_bytes=VMEM_LIMIT_BYTES)


def _mixer_prompt(x, mw, chunk):
    batch, seq, d = x.shape
    rows = MIXER_ROWS
    assert seq % rows == 0 and rows % chunk == 0
    kern = functools.partial(_mixer_prompt_kernel, rows=rows, chunk=chunk)
    return pl.pallas_call(
        kern,
        grid=(batch, seq // rows),
        in_specs=[pl.BlockSpec((None, rows, d), lambda b, i: (b, i, 0))]
        + [_resident(w.shape) for w in mw],
        out_specs=[pl.BlockSpec((None, rows, d), lambda b, i: (b, i, 0)),
                   pl.BlockSpec((None, HIST, d), lambda b, i: (b, 0, 0))],
        out_shape=[jax.ShapeDtypeStruct((batch, seq, d), F32),
                   jax.ShapeDtypeStruct((batch, HIST, d), F32)],
        scratch_shapes=[pltpu.VMEM((COL_BLOCKS, HIST_PAD + rows, LANES), F32)],
        compiler_params=_params(("arbitrary", "arbitrary")),
        name="mixer_prompt",
    )(x, *mw)


def _mixer_sample(x, hist, mw, chunk):
    batch, seq, d = x.shape
    total = batch * seq
    kern = functools.partial(_mixer_sample_kernel, nseq=batch, rows=seq, chunk=chunk)
    h, cs, vn = pl.pallas_call(
        kern,
        grid=(1,),
        in_specs=[_full((total, d)), _full(hist.shape)] + [_resident(w.shape) for w in mw],
        out_specs=[_full((total, d)), _full((batch, HIST, d)), _full((total, d))],
        out_shape=[jax.ShapeDtypeStruct((total, d), F32),
                   jax.ShapeDtypeStruct((batch, HIST, d), F32),
                   jax.ShapeDtypeStruct((total, d), F32)],
        scratch_shapes=[pltpu.VMEM((batch * COL_BLOCKS, HIST_PAD + seq, LANES), F32)],
        compiler_params=_params(("arbitrary",)),
        name="mixer_sample",
    )(x.reshape(total, d), hist, *mw)
    return h.reshape(batch, seq, d), cs, vn.reshape(batch, seq, d)


def _ffn(h, p, fw, rows):
    batch, seq, d = h.shape
    total = batch * seq
    assert total % rows == 0
    y = pl.pallas_call(
        _ffn_kernel,
        grid=(total // rows,),
        in_specs=[pl.BlockSpec((rows, d), lambda i: (i, 0)),
                  pl.BlockSpec((rows, PLE_DIM), lambda i: (i, 0))]
        + [_resident(w.shape) for w in fw],
        out_specs=pl.BlockSpec((rows, d), lambda i: (i, 0)),
        out_shape=jax.ShapeDtypeStruct((total, d), F32),
        compiler_params=_params(("arbitrary",)),
        name="ffn_rows%d" % rows,
    )(h.reshape(total, d), p.reshape(total, PLE_DIM), *fw)
    return y.reshape(batch, seq, d)


def kernel(x_prompt, x_sample, cache_conv, p_prompt, p_sample, g_mix, w_in, g_sgv, b_sgv, w_sg, b_sg, w_a, w_dw, b_dw, g_cln, b_cln, w_b, w_out, g_ffn, w_up, w_down, g_ple, w_ple_gate, w_ple, g_final):
    depth = w_in.shape[0]
    assert depth == 1
    row = lambda t: t.reshape(1, -1)
    dec_seq = x_sample.shape[1]
    hp, hs = x_prompt, x_sample
    conv_p, conv_s, v_s = [], [], []
    for i in range(depth):
        bias = jnp.repeat(jnp.transpose(b_sg[i]), SG_GROUP_DIM, axis=1)
        shared = (row(g_mix[i]), w_in[i].astype(BF16), row(g_sgv[i]), row(b_sgv[i]))
        tail = (bias, w_a[i].astype(BF16), w_dw[i], row(b_dw[i]), row(g_cln[i]), row(b_cln[i]),
                w_b[i].astype(BF16), w_out[i].astype(BF16))
        tril = lambda n: jnp.tril(w_sg[i][:, :n, :n]).astype(BF16)
        fw = (row(g_ffn[i]), w_up[i].astype(BF16), w_down[i].astype(BF16), row(g_ple[i]),
              w_ple_gate[i].astype(BF16), w_ple[i].astype(BF16), row(g_final))

        h1p, cp = _mixer_prompt(hp, shared + (tril(SG_CHUNK),) + tail, SG_CHUNK)
        h1s, cs, vs = _mixer_sample(hs, cache_conv[i], shared + (tril(dec_seq),) + tail, dec_seq)
        hp = _ffn(h1p, p_prompt[i], fw, FFN_ROWS)
        hs = _ffn(h1s, p_sample[i], fw, hs.shape[0] * hs.shape[1])
        conv_p.append(cp)
        conv_s.append(cs)
        v_s.append(vs)
    return (hp, hs, jnp.stack(conv_p), jnp.stack(conv_s), jnp.stack(v_s))
```

```python
import functools

import jax
import jax.numpy as jnp
from jax import lax
from jax.experimental import pallas as pl
from jax.experimental.pallas import tpu as pltpu

D_MODEL = 1024
SG_CHUNK = 128
SG_GROUPS = 8
SG_GROUP_DIM = D_MODEL // SG_GROUPS
CONV_WIDTH = 31
HIST = CONV_WIDTH - 1
D_FF = 4 * D_MODEL
PLE_DIM = 256
EPS = 1e-6

LANES = 128
SUBLANES = 8
HEAD_ROWS = 16
COL_BLOCKS = D_MODEL // LANES
HIST_PAD = 32
HIST_OFF = HIST_PAD - HIST
CONV_ROW_BLOCK = 128

MIXER_ROWS = 256
FFN_ROWS = 512
STAGE_ROWS = 512
STAGE_COLS = 1024
VMEM_LIMIT_BYTES = 56 * 1024 * 1024

BF16 = jnp.bfloat16
F32 = jnp.float32


def _rms(x, g):
    ms = jnp.mean(x * x, axis=-1, keepdims=True)
    return x * lax.rsqrt(ms + EPS) * g


def _ln(x, g, b):
    mu = jnp.mean(x, axis=-1, keepdims=True)
    xc = x - mu
    y = xc * lax.rsqrt(jnp.mean(xc * xc, axis=-1, keepdims=True) + EPS)
    return y * g + b


def _dot(a, b):
    return jnp.dot(a, b, preferred_element_type=F32)


def _load_weights_bf16(pairs, stage, sem):
    pieces = []
    for src, dst in pairs:
        k, n = src.shape
        assert n % STAGE_COLS == 0
        for r0 in range(0, k, STAGE_ROWS):
            for c0 in range(0, n, STAGE_COLS):
                pieces.append((src, dst, r0, min(STAGE_ROWS, k - r0), c0))

    def copy(i):
        src, _, r0, rr, c0 = pieces[i]
        return pltpu.make_async_copy(src.at[pl.ds(r0, rr), pl.ds(c0, STAGE_COLS)],
                                     stage.at[i % 2, pl.ds(0, rr), :], sem.at[i % 2])

    copy(0).start()
    for i, (_, dst, r0, rr, c0) in enumerate(pieces):
        if i + 1 < len(pieces):
            copy(i + 1).start()
        copy(i).wait()
        dst[pl.ds(r0, rr), pl.ds(c0, STAGE_COLS)] = stage[i % 2, 0:rr, :].astype(BF16)


def _conv_piece(cbuf, seq, wdw_ref, j, r0, rb):
    lanes = slice(j * LANES, (j + 1) * LANES)
    acc = None
    for k in range(CONV_WIDTH):
        w = wdw_ref[k:k + 1, lanes]
        term = cbuf[seq * COL_BLOCKS + j, pl.ds(HIST_OFF + k + r0, rb), :] * w
        acc = term if acc is None else acc + term
    return acc


def _depthwise_conv(cbuf, seq, rows, wdw_ref):
    return jnp.concatenate([_conv_piece(cbuf, seq, wdw_ref, j, 0, rows)
                            for j in range(COL_BLOCKS)], axis=1)


def _all_bits(x):
    bits = pltpu.bitcast(x, jnp.uint32)
    out = None
    for r in range(0, x.shape[0], SUBLANES):
        for c in range(0, x.shape[1], LANES):
            blk = bits[r:r + SUBLANES, c:c + LANES]
            out = blk if out is None else out | blk
    return out


def _after(x, token):
    if token is None:
        return x
    reps = (x.shape[0] // SUBLANES, x.shape[1] // LANES)
    return x + jnp.tile(pltpu.bitcast(token, F32), reps)


FRONT_GROUPS = 7
GROUP_PIECES = (4, 2, 2, 4, 4, 0, 0)


def _front(x, n_scr, gmix, win, gsgv, bsgv, wsg, bsg, wa, chunk, emit_group=None):
    if emit_group is None:
        emit_group = lambda k: None
    d = D_MODEL
    nf = _rms(x, gmix[...])
    n_scr[...] = nf.astype(BF16)
    head = nf[0:HEAD_ROWS]

    def lhs(token):
        if token is None:
            return n_scr[...]
        return jnp.concatenate([_after(head, token).astype(BF16), n_scr[HEAD_ROWS:, :]], axis=0)

    def cast_after(y, token):
        if token is None:
            return y.astype(BF16)
        return jnp.concatenate([_after(y[0:HEAD_ROWS], token).astype(BF16),
                                y[HEAD_ROWS:].astype(BF16)], axis=0)

    v = _dot(n_scr[...], win[:, d:2 * d])
    u = _dot(lhs(emit_group(0)), win[:, 0:d])
    vn = _ln(v, gsgv[...], bsgv[...])
    vnb = vn.astype(BF16)
    nch = x.shape[0] // chunk
    per_group = []
    for g in range(SG_GROUPS):
        lanes = slice(g * SG_GROUP_DIM, (g + 1) * SG_GROUP_DIM)
        vg = jnp.concatenate([vnb[c * chunk:(c + 1) * chunk, lanes] for c in range(nch)], axis=1)
        per_group.append(_dot(wsg[g], vg))
    bias = bsg[0:chunk, :]
    s_rows = []
    for c in range(nch):
        lanes = slice(c * SG_GROUP_DIM, (c + 1) * SG_GROUP_DIM)
        s_rows.append(jnp.concatenate([sg[:, lanes] for sg in per_group], axis=1) + bias)
    s = jnp.concatenate(s_rows, axis=0)

    ca = _dot(lhs(emit_group(1)), win[:, 2 * d:3 * d])
    a_mix = _dot(cast_after(u * s, emit_group(2)), wa[...])
    cb = _dot(lhs(emit_group(3)), win[:, 3 * d:4 * d])
    ga = _dot(lhs(emit_group(4)), win[:, 4 * d:5 * d])
    c = ca * jax.nn.sigmoid(cb)
    gb = _dot(lhs(emit_group(5)), win[:, 5 * d:6 * d])
    emit_group(6)
    return vn, jax.nn.sigmoid(ga) * a_mix, jax.nn.sigmoid(gb), c


def _back(x, a_term, gate_b, dw, gcln, bcln, wb, wout):
    bb = jax.nn.silu(_ln(dw, gcln[...], bcln[...])).astype(BF16)
    mix = (a_term + gate_b * _dot(bb, wb[...])).astype(BF16)
    return x + _dot(mix, wout[...])


def _mixer_prompt_kernel(xc_ref, xp_ref, zero_ref, gmix, win_hbm, gsgv, bsgv, wsg, bsg, wa_hbm,
                         wdw, bdw, gcln, bcln, wb_hbm, wout_hbm, h_ref, cs_ref,
                         win, wa, wb, wout, stage, sem, n_scr, dw_scr, cbuf, a_ring, g_ring,
                         *, rows, chunk, tiles_per_seq):
    t = pl.program_id(0)
    cur = lax.rem(t, 2)
    prev = 1 - cur

    @pl.when(t == 0)
    def _():
        _load_weights_bf16([(win_hbm, win), (wa_hbm, wa), (wb_hbm, wb), (wout_hbm, wout)],
                           stage, sem)
        cbuf[...] = jnp.zeros(cbuf.shape, F32)
        a_ring[...] = jnp.zeros(a_ring.shape, F32)
        g_ring[...] = jnp.zeros(g_ring.shape, F32)

    pieces = [(j, r0) for j in range(COL_BLOCKS) for r0 in range(0, rows, CONV_ROW_BLOCK)]
    assert sum(GROUP_PIECES) == len(pieces)
    bounds = [sum(GROUP_PIECES[:g]) for g in range(FRONT_GROUPS + 1)]

    def emit_group(g):
        bits = None
        for j, r0 in pieces[bounds[g]:bounds[g + 1]]:
            piece = _conv_piece(cbuf, prev, wdw, j, r0, CONV_ROW_BLOCK)
            dw_scr[r0:r0 + CONV_ROW_BLOCK, j * LANES:(j + 1) * LANES] = piece
            bits = _all_bits(piece) if bits is None else bits | _all_bits(piece)
        return None if bits is None else bits & zero_ref[...]

    _, a_term, gate_b, c = _front(xc_ref[...], n_scr, gmix, win, gsgv, bsgv, wsg, bsg, wa, chunk,
                                  emit_group=emit_group)
    dw = dw_scr[...] + bdw[...]
    a_ring[cur] = a_term
    g_ring[cur] = gate_b
    first = lax.rem(t, tiles_per_seq) == 0
    for j in range(COL_BLOCKS):
        lanes = slice(j * LANES, (j + 1) * LANES)
        tail = cbuf[prev * COL_BLOCKS + j, rows:rows + HIST_PAD, :]
        cbuf[cur * COL_BLOCKS + j, 0:HIST_PAD, :] = jnp.where(first, 0.0, tail)
        cbuf[cur * COL_BLOCKS + j, HIST_PAD:HIST_PAD + rows, :] = c[:, lanes]

    for j in range(COL_BLOCKS):
        lanes = slice(j * LANES, (j + 1) * LANES)
        cs_ref[:, lanes] = cbuf[prev * COL_BLOCKS + j, rows + HIST_OFF:rows + HIST_PAD, :]
    h_ref[...] = _back(xp_ref[...], a_ring[prev], g_ring[prev], dw, gcln, bcln, wb, wout)


def _mixer_sample_kernel(x_ref, hist_ref, gmix, win_hbm, gsgv, bsgv, wsg, bsg, wa_hbm, wdw, bdw,
                         gcln, bcln, wb_hbm, wout_hbm, h_ref, cs_ref, vn_ref,
                         win, wa, wb, wout, stage, sem, n_scr, cbuf, *, nseq, rows, chunk):
    _load_weights_bf16([(win_hbm, win), (wa_hbm, wa), (wb_hbm, wb), (wout_hbm, wout)], stage, sem)
    x = x_ref[...]
    vn, a_term, gate_b, c = _front(x, n_scr, gmix, win, gsgv, bsgv, wsg, bsg, wa, chunk)
    vn_ref[...] = vn
    for q in range(nseq):
        for j in range(COL_BLOCKS):
            lanes = slice(j * LANES, (j + 1) * LANES)
            cbuf[q * COL_BLOCKS + j, HIST_OFF:HIST_PAD, :] = hist_ref[q, :, lanes]
            cbuf[q * COL_BLOCKS + j, HIST_PAD:HIST_PAD + rows, :] = c[q * rows:(q + 1) * rows, lanes]
    dw = jnp.concatenate([_depthwise_conv(cbuf, q, rows, wdw) for q in range(nseq)], axis=0)
    for q in range(nseq):
        for j in range(COL_BLOCKS):
            lanes = slice(j * LANES, (j + 1) * LANES)
            cs_ref[q, :, lanes] = cbuf[q * COL_BLOCKS + j, rows + HIST_OFF:rows + HIST_PAD, :]
    h_ref[...] = _back(x, a_term, gate_b, dw + bdw[...], gcln, bcln, wb, wout)


def _ffn_kernel(h_ref, p_ref, gffn, wup_hbm, wdown_hbm, gple, wgate_hbm, wple_hbm, gfinal, y_ref,
                wup, wdown, wgate, wple, stage, sem):
    @pl.when(pl.program_id(0) == 0)
    def _():
        _load_weights_bf16([(wup_hbm, wup), (wdown_hbm, wdown), (wgate_hbm, wgate),
                            (wple_hbm, wple)], stage, sem)

    h = h_ref[...]
    f = _dot(_rms(h, gffn[...]).astype(BF16), wup[...])
    r = jnp.square(jnp.maximum(f, 0.0)).astype(BF16)
    h = h + _dot(r, wdown[...])
    gate = jax.nn.sigmoid(_dot(_rms(h, gple[...]).astype(BF16), wgate[...]))
    e = _dot(p_ref[...].astype(BF16), wple[...])
    h = h + gate * e
    y_ref[...] = _rms(h, gfinal[...])


def _resident(shape):
    zeros = (0,) * len(shape)
    return pl.BlockSpec(shape, lambda *_: zeros, pipeline_mode=pl.Buffered(1))


def _full(shape):
    zeros = (0,) * len(shape)
    return pl.BlockSpec(shape, lambda *_: zeros)


def _weight_specs(weights):
    return [pl.BlockSpec(memory_space=pl.ANY) if _is_matrix(w) else _resident(w.shape)
            for w in weights]


def _is_matrix(w):
    return w.ndim == 2 and w.dtype == F32 and w.shape[0] >= PLE_DIM and w.shape[1] >= STAGE_COLS


def _weight_scratch(weights):
    return ([pltpu.VMEM(w.shape, BF16) for w in weights if _is_matrix(w)]
            + [pltpu.VMEM((2, STAGE_ROWS, STAGE_COLS), F32), pltpu.SemaphoreType.DMA((2,))])


def _params(semantics, flags=None):
    return pltpu.CompilerParams(dimension_semantics=semantics, vmem_limit_bytes=VMEM_LIMIT_BYTES,
                                flags=flags)


def _mixer_prompt(x, mw, chunk):
    batch, seq, d = x.shape
    rows = MIXER_ROWS
    assert seq % rows == 0 and rows % chunk == 0
    tiles_per_seq = seq // rows
    tiles = batch * tiles_per_seq
    kern = functools.partial(_mixer_prompt_kernel, rows=rows, chunk=chunk,
                             tiles_per_seq=tiles_per_seq)
    cur_tile = lambda t: (jnp.minimum(t, tiles - 1), 0)
    prev_tile = lambda t: (jnp.maximum(t - 1, 0), 0)
    h, cs = pl.pallas_call(
        kern,
        grid=(tiles + 1,),
        in_specs=[pl.BlockSpec((rows, d), cur_tile), pl.BlockSpec((rows, d), prev_tile),
                  _resident((SUBLANES, LANES))]
        + _weight_specs(mw),
        out_specs=[pl.BlockSpec((rows, d), prev_tile),
                   pl.BlockSpec((None, HIST, d),
                                lambda t: (jnp.maximum(t - 1, 0) // tiles_per_seq, 0, 0))],
        out_shape=[jax.ShapeDtypeStruct((tiles * rows, d), F32),
                   jax.ShapeDtypeStruct((batch, HIST, d), F32)],
        scratch_shapes=_weight_scratch(mw)
        + [pltpu.VMEM((rows, d), BF16),
           pltpu.VMEM((rows, d), F32),
           pltpu.VMEM((2 * COL_BLOCKS, HIST_PAD + rows, LANES), F32),
           pltpu.VMEM((2, rows, d), F32),
           pltpu.VMEM((2, rows, d), F32)],
        compiler_params=_params(("arbitrary",)),
        name="mixer_prompt",
    )(x.reshape(tiles * rows, d), x.reshape(tiles * rows, d),
      jnp.zeros((SUBLANES, LANES), jnp.uint32), *mw)
    return h.reshape(batch, seq, d), cs


def _mixer_sample(x, hist, mw, chunk):
    batch, seq, d = x.shape
    total = batch * seq
    kern = functools.partial(_mixer_sample_kernel, nseq=batch, rows=seq, chunk=chunk)
    h, cs, vn = pl.pallas_call(
        kern,
        grid=(1,),
        in_specs=[_full((total, d)), _full(hist.shape)] + _weight_specs(mw),
        out_specs=[_full((total, d)), _full((batch, HIST, d)), _full((total, d))],
        out_shape=[jax.ShapeDtypeStruct((total, d), F32),
                   jax.ShapeDtypeStruct((batch, HIST, d), F32),
                   jax.ShapeDtypeStruct((total, d), F32)],
        scratch_shapes=_weight_scratch(mw)
        + [pltpu.VMEM((total, d), BF16),
           pltpu.VMEM((batch * COL_BLOCKS, HIST_PAD + seq, LANES), F32)],
        compiler_params=_params(("arbitrary",)),
        name="mixer_sample",
    )(x.reshape(total, d), hist, *mw)
    return h.reshape(batch, seq, d), cs, vn.reshape(batch, seq, d)


def _ffn(h, p, fw, rows):
    batch, seq, d = h.shape
    total = batch * seq
    assert total % rows == 0
    y = pl.pallas_call(
        _ffn_kernel,
        grid=(total // rows,),
        in_specs=[pl.BlockSpec((rows, d), lambda i: (i, 0)),
                  pl.BlockSpec((rows, PLE_DIM), lambda i: (i, 0))]
        + _weight_specs(fw),
        out_specs=pl.BlockSpec((rows, d), lambda i: (i, 0)),
        out_shape=jax.ShapeDtypeStruct((total, d), F32),
        scratch_shapes=_weight_scratch(fw),
        compiler_params=_params(("arbitrary",)),
        name="ffn_rows%d" % rows,
    )(h.reshape(total, d), p.reshape(total, PLE_DIM), *fw)
    return y.reshape(batch, seq, d)


def kernel(x_prompt, x_sample, cache_conv, p_prompt, p_sample, g_mix, w_in, g_sgv, b_sgv, w_sg, b_sg, w_a, w_dw, b_dw, g_cln, b_cln, w_b, w_out, g_ffn, w_up, w_down, g_ple, w_ple_gate, w_ple, g_final):
    depth = w_in.shape[0]
    assert depth == 1
    row = lambda t: t.reshape(1, -1)
    dec_seq = x_sample.shape[1]
    hp, hs = x_prompt, x_sample
    conv_p, conv_s, v_s = [], [], []
    for i in range(depth):
        bias = jnp.repeat(jnp.transpose(b_sg[i]), SG_GROUP_DIM, axis=1)
        shared = (row(g_mix[i]), w_in[i], row(g_sgv[i]), row(b_sgv[i]))
        tail = (bias, w_a[i], w_dw[i], row(b_dw[i]), row(g_cln[i]), row(b_cln[i]),
                w_b[i], w_out[i])
        tril = lambda n: jnp.tril(w_sg[i][:, :n, :n]).astype(BF16)
        fw = (row(g_ffn[i]), w_up[i], w_down[i], row(g_ple[i]), w_ple_gate[i], w_ple[i],
              row(g_final))

        h1p, cp = _mixer_prompt(hp, shared + (tril(SG_CHUNK),) + tail, SG_CHUNK)
        h1s, cs, vs = _mixer_sample(hs, cache_conv[i], shared + (tril(dec_seq),) + tail, dec_seq)
        hp = _ffn(h1p, p_prompt[i], fw, FFN_ROWS)
        hs = _ffn(h1s, p_sample[i], fw, hs.shape[0] * hs.shape[1])
        conv_p.append(cp)
        conv_s.append(cs)
        v_s.append(vs)
    return (hp, hs, jnp.stack(conv_p), jnp.stack(conv_s), jnp.stack(v_s))
```

```python
import functools

import jax
import jax.numpy as jnp
from jax import lax
from jax.experimental import pallas as pl
from jax.experimental.pallas import tpu as pltpu

D_MODEL = 1024
SG_CHUNK = 128
SG_GROUPS = 8
SG_GROUP_DIM = D_MODEL // SG_GROUPS
CONV_WIDTH = 31
HIST = CONV_WIDTH - 1
D_FF = 4 * D_MODEL
PLE_DIM = 256
EPS = 1e-6

LANES = 128
SUBLANES = 8
HEAD_ROWS = 16
COL_BLOCKS = D_MODEL // LANES
HIST_PAD = 32
HIST_OFF = HIST_PAD - HIST
CONV_ROW_BLOCK = 128

MIXER_ROWS = 256
FFN_ROWS = 512
STAGE_ROWS = 512
STAGE_COLS = 1024
VMEM_LIMIT_BYTES = 56 * 1024 * 1024

BF16 = jnp.bfloat16
F32 = jnp.float32


def _rms(x, g):
    ms = jnp.mean(x * x, axis=-1, keepdims=True)
    return x * lax.rsqrt(ms + EPS) * g


def _ln(x, g, b):
    mu = jnp.mean(x, axis=-1, keepdims=True)
    xc = x - mu
    y = xc * lax.rsqrt(jnp.mean(xc * xc, axis=-1, keepdims=True) + EPS)
    return y * g + b


def _dot(a, b):
    return jnp.dot(a, b, preferred_element_type=F32)


def _load_weights_bf16(pairs, stage, sem):
    pieces = []
    for src, dst in pairs:
        k, n = src.shape
        assert n % STAGE_COLS == 0
        for r0 in range(0, k, STAGE_ROWS):
            for c0 in range(0, n, STAGE_COLS):
                pieces.append((src, dst, r0, min(STAGE_ROWS, k - r0), c0))

    def copy(i):
        src, _, r0, rr, c0 = pieces[i]
        return pltpu.make_async_copy(src.at[pl.ds(r0, rr), pl.ds(c0, STAGE_COLS)],
                                     stage.at[i % 2, pl.ds(0, rr), :], sem.at[i % 2])

    copy(0).start()
    for i, (_, dst, r0, rr, c0) in enumerate(pieces):
        if i + 1 < len(pieces):
            copy(i + 1).start()
        copy(i).wait()
        dst[pl.ds(r0, rr), pl.ds(c0, STAGE_COLS)] = stage[i % 2, 0:rr, :].astype(BF16)


def _conv_piece(cbuf, seq, wdw_ref, j, r0, rb):
    lanes = slice(j * LANES, (j + 1) * LANES)
    nb = rb // SUBLANES
    nq = -(-CONV_WIDTH // SUBLANES)
    acc = [None] * nb
    for s in range(SUBLANES):
        ws = [wdw_ref[q * SUBLANES + s:q * SUBLANES + s + 1, lanes]
              if q * SUBLANES + s < CONV_WIDTH else None for q in range(nq)]
        for m in range(nb + nq - 1):
            taps = [q for q in range(nq) if ws[q] is not None and 0 <= m - q < nb]
            if not taps:
                continue
            rows = cbuf[seq * COL_BLOCKS + j, pl.ds(HIST_OFF + r0 + m * SUBLANES + s, SUBLANES), :]
            for q in taps:
                term = rows * ws[q]
                acc[m - q] = term if acc[m - q] is None else acc[m - q] + term
    return jnp.concatenate(acc, axis=0)


def _depthwise_conv(cbuf, seq, rows, wdw_ref):
    return jnp.concatenate([_conv_piece(cbuf, seq, wdw_ref, j, 0, rows)
                            for j in range(COL_BLOCKS)], axis=1)


def _all_bits(x):
    bits = pltpu.bitcast(x, jnp.uint32)
    out = None
    for r in range(0, x.shape[0], SUBLANES):
        for c in range(0, x.shape[1], LANES):
            blk = bits[r:r + SUBLANES, c:c + LANES]
            out = blk if out is None else out | blk
    return out


def _after(x, token):
    if token is None:
        return x
    reps = (x.shape[0] // SUBLANES, x.shape[1] // LANES)
    return x + jnp.tile(pltpu.bitcast(token, F32), reps)


FRONT_GROUPS = 7
GROUP_PIECES = (4, 2, 2, 4, 4, 0, 0)


def _front(x, n_scr, gmix, win, gsgv, bsgv, wsg, bsg, wa, chunk, emit_group=None):
    if emit_group is None:
        emit_group = lambda k: None
    d = D_MODEL
    nf = _rms(x, gmix[...])
    n_scr[...] = nf.astype(BF16)
    head = nf[0:HEAD_ROWS]

    def lhs(token):
        if token is None:
            return n_scr[...]
        return jnp.concatenate([_after(head, token).astype(BF16), n_scr[HEAD_ROWS:, :]], axis=0)

    def cast_after(y, token):
        if token is None:
            return y.astype(BF16)
        return jnp.concatenate([_after(y[0:HEAD_ROWS], token).astype(BF16),
                                y[HEAD_ROWS:].astype(BF16)], axis=0)

    v = _dot(n_scr[...], win[:, d:2 * d])
    u = _dot(lhs(emit_group(0)), win[:, 0:d])
    vn = _ln(v, gsgv[...], bsgv[...])
    vnb = vn.astype(BF16)
    nch = x.shape[0] // chunk
    per_group = []
    for g in range(SG_GROUPS):
        lanes = slice(g * SG_GROUP_DIM, (g + 1) * SG_GROUP_DIM)
        vg = jnp.concatenate([vnb[c * chunk:(c + 1) * chunk, lanes] for c in range(nch)], axis=1)
        per_group.append(_dot(wsg[g], vg))
    bias = bsg[0:chunk, :]
    s_rows = []
    for c in range(nch):
        lanes = slice(c * SG_GROUP_DIM, (c + 1) * SG_GROUP_DIM)
        s_rows.append(jnp.concatenate([sg[:, lanes] for sg in per_group], axis=1) + bias)
    s = jnp.concatenate(s_rows, axis=0)

    ca = _dot(lhs(emit_group(1)), win[:, 2 * d:3 * d])
    a_mix = _dot(cast_after(u * s, emit_group(2)), wa[...])
    cb = _dot(lhs(emit_group(3)), win[:, 3 * d:4 * d])
    ga = _dot(lhs(emit_group(4)), win[:, 4 * d:5 * d])
    c = ca * jax.nn.sigmoid(cb)
    gb = _dot(lhs(emit_group(5)), win[:, 5 * d:6 * d])
    emit_group(6)
    return vn, jax.nn.sigmoid(ga) * a_mix, jax.nn.sigmoid(gb), c


def _back(x, a_term, gate_b, dw, gcln, bcln, wb, wout):
    bb = jax.nn.silu(_ln(dw, gcln[...], bcln[...])).astype(BF16)
    mix = (a_term + gate_b * _dot(bb, wb[...])).astype(BF16)
    return x + _dot(mix, wout[...])


def _mixer_kernel(xc_ref, xp_ref, zero_ref, xs_ref, hist_ref, wsg_s, gmix, win_hbm, gsgv, bsgv, wsg,
                  bsg, wa_hbm, wdw, bdw, gcln, bcln, wb_hbm, wout_hbm,
                  h_ref, cs_ref, hs_ref, css_ref, vns_ref,
                  win, wa, wb, wout, stage, sem, n_scr, dw_scr, cbuf, a_ring, g_ring, cbuf_s,
                  *, rows, chunk, tiles_per_seq, tiles, seqs_s, rows_s):
    t = pl.program_id(0)
    cur = lax.rem(t, 2)
    prev = 1 - cur

    @pl.when(t == 0)
    def _():
        _load_weights_bf16([(win_hbm, win), (wa_hbm, wa), (wb_hbm, wb), (wout_hbm, wout)],
                           stage, sem)
        cbuf[...] = jnp.zeros(cbuf.shape, F32)
        a_ring[...] = jnp.zeros(a_ring.shape, F32)
        g_ring[...] = jnp.zeros(g_ring.shape, F32)

    pieces = [(j, r0) for j in range(COL_BLOCKS) for r0 in range(0, rows, CONV_ROW_BLOCK)]
    assert sum(GROUP_PIECES) == len(pieces)
    bounds = [sum(GROUP_PIECES[:g]) for g in range(FRONT_GROUPS + 1)]

    def emit_group(g):
        bits = None
        for j, r0 in pieces[bounds[g]:bounds[g + 1]]:
            piece = _conv_piece(cbuf, prev, wdw, j, r0, CONV_ROW_BLOCK)
            dw_scr[r0:r0 + CONV_ROW_BLOCK, j * LANES:(j + 1) * LANES] = piece
            bits = _all_bits(piece) if bits is None else bits | _all_bits(piece)
        return None if bits is None else bits & zero_ref[...]

    _, a_term, gate_b, c = _front(xc_ref[...], n_scr, gmix, win, gsgv, bsgv, wsg, bsg, wa, chunk,
                                  emit_group=emit_group)
    dw = dw_scr[...] + bdw[...]
    a_ring[cur] = a_term
    g_ring[cur] = gate_b
    first = lax.rem(t, tiles_per_seq) == 0
    for j in range(COL_BLOCKS):
        lanes = slice(j * LANES, (j + 1) * LANES)
        tail = cbuf[prev * COL_BLOCKS + j, rows:rows + HIST_PAD, :]
        cbuf[cur * COL_BLOCKS + j, 0:HIST_PAD, :] = jnp.where(first, 0.0, tail)
        cbuf[cur * COL_BLOCKS + j, HIST_PAD:HIST_PAD + rows, :] = c[:, lanes]

    for j in range(COL_BLOCKS):
        lanes = slice(j * LANES, (j + 1) * LANES)
        cs_ref[:, lanes] = cbuf[prev * COL_BLOCKS + j, rows + HIST_OFF:rows + HIST_PAD, :]
    h_ref[...] = _back(xp_ref[...], a_ring[prev], g_ring[prev], dw, gcln, bcln, wb, wout)

    @pl.when(t == tiles)
    def _():
        x = xs_ref[...]
        vn, a_s, g_s, c_s = _front(x, n_scr, gmix, win, gsgv, bsgv, wsg_s, bsg, wa, rows_s)
        vns_ref[...] = vn
        for q in range(seqs_s):
            for j in range(COL_BLOCKS):
                lanes = slice(j * LANES, (j + 1) * LANES)
                cbuf_s[q * COL_BLOCKS + j, HIST_OFF:HIST_PAD, :] = hist_ref[q, :, lanes]
                cbuf_s[q * COL_BLOCKS + j, HIST_PAD:HIST_PAD + rows_s, :] = (
                    c_s[q * rows_s:(q + 1) * rows_s, lanes])
        dw_s = jnp.concatenate([_depthwise_conv(cbuf_s, q, rows_s, wdw) for q in range(seqs_s)],
                               axis=0)
        for q in range(seqs_s):
            for j in range(COL_BLOCKS):
                lanes = slice(j * LANES, (j + 1) * LANES)
                css_ref[q, :, lanes] = cbuf_s[q * COL_BLOCKS + j,
                                              rows_s + HIST_OFF:rows_s + HIST_PAD, :]
        hs_ref[...] = _back(x, a_s, g_s, dw_s + bdw[...], gcln, bcln, wb, wout)


def _ffn_rows(h, p, gffn, wup, wdown, gple, wgate, wple, gfinal):
    f = _dot(_rms(h, gffn[...]).astype(BF16), wup[...])
    r = jnp.square(jnp.maximum(f, 0.0)).astype(BF16)
    h = h + _dot(r, wdown[...])
    gate = jax.nn.sigmoid(_dot(_rms(h, gple[...]).astype(BF16), wgate[...]))
    e = _dot(p.astype(BF16), wple[...])
    h = h + gate * e
    return _rms(h, gfinal[...])


def _ffn_kernel(h_ref, p_ref, hs_ref, ps_ref, gffn, wup_hbm, wdown_hbm, gple, wgate_hbm, wple_hbm,
                gfinal, y_ref, ys_ref, wup, wdown, wgate, wple, stage, sem):
    @pl.when(pl.program_id(0) == 0)
    def _():
        _load_weights_bf16([(wup_hbm, wup), (wdown_hbm, wdown), (wgate_hbm, wgate),
                            (wple_hbm, wple)], stage, sem)

    y_ref[...] = _ffn_rows(h_ref[...], p_ref[...], gffn, wup, wdown, gple, wgate, wple, gfinal)

    @pl.when(pl.program_id(0) == pl.num_programs(0) - 1)
    def _():
        ys_ref[...] = _ffn_rows(hs_ref[...], ps_ref[...], gffn, wup, wdown, gple, wgate, wple,
                                gfinal)


def _resident(shape):
    zeros = (0,) * len(shape)
    return pl.BlockSpec(shape, lambda *_: zeros, pipeline_mode=pl.Buffered(1))


def _full(shape):
    zeros = (0,) * len(shape)
    return pl.BlockSpec(shape, lambda *_: zeros)


def _weight_specs(weights):
    return [pl.BlockSpec(memory_space=pl.ANY) if _is_matrix(w) else _resident(w.shape)
            for w in weights]


def _is_matrix(w):
    return w.ndim == 2 and w.dtype == F32 and w.shape[0] >= PLE_DIM and w.shape[1] >= STAGE_COLS


def _weight_scratch(weights):
    return ([pltpu.VMEM(w.shape, BF16) for w in weights if _is_matrix(w)]
            + [pltpu.VMEM((2, STAGE_ROWS, STAGE_COLS), F32), pltpu.SemaphoreType.DMA((2,))])


def _params(semantics):
    return pltpu.CompilerParams(dimension_semantics=semantics, vmem_limit_bytes=VMEM_LIMIT_BYTES)


def _mixer(x, xs, hist, wsg_s, mw, chunk):
    batch, seq, d = x.shape
    seqs_s, rows_s, _ = xs.shape
    rows = MIXER_ROWS
    total_s = seqs_s * rows_s
    assert seq % rows == 0 and rows % chunk == 0 and total_s == rows
    tiles_per_seq = seq // rows
    tiles = batch * tiles_per_seq
    kern = functools.partial(_mixer_kernel, rows=rows, chunk=chunk, tiles_per_seq=tiles_per_seq,
                             tiles=tiles, seqs_s=seqs_s, rows_s=rows_s)
    cur_tile = lambda t: (jnp.minimum(t, tiles - 1), 0)
    prev_tile = lambda t: (jnp.maximum(t - 1, 0), 0)
    x2 = x.reshape(tiles * rows, d)
    h, cs, hs, css, vns = pl.pallas_call(
        kern,
        grid=(tiles + 1,),
        in_specs=[pl.BlockSpec((rows, d), cur_tile), pl.BlockSpec((rows, d), prev_tile),
                  _resident((SUBLANES, LANES)), _resident((total_s, d)), _resident(hist.shape),
                  _resident(wsg_s.shape)]
        + _weight_specs(mw),
        out_specs=[pl.BlockSpec((rows, d), prev_tile),
                   pl.BlockSpec((None, HIST, d),
                                lambda t: (jnp.maximum(t - 1, 0) // tiles_per_seq, 0, 0)),
                   _full((total_s, d)), _full((seqs_s, HIST, d)), _full((total_s, d))],
        out_shape=[jax.ShapeDtypeStruct((tiles * rows, d), F32),
                   jax.ShapeDtypeStruct((batch, HIST, d), F32),
                   jax.ShapeDtypeStruct((total_s, d), F32),
                   jax.ShapeDtypeStruct((seqs_s, HIST, d), F32),
                   jax.ShapeDtypeStruct((total_s, d), F32)],
        scratch_shapes=_weight_scratch(mw)
        + [pltpu.VMEM((rows, d), BF16),
           pltpu.VMEM((rows, d), F32),
           pltpu.VMEM((2 * COL_BLOCKS, HIST_PAD + rows, LANES), F32),
           pltpu.VMEM((2, rows, d), F32),
           pltpu.VMEM((2, rows, d), F32),
           pltpu.VMEM((seqs_s * COL_BLOCKS, HIST_PAD + rows_s, LANES), F32)],
        compiler_params=_params(("arbitrary",)),
        name="mixer",
    )(x2, x2, jnp.zeros((SUBLANES, LANES), jnp.uint32), xs.reshape(total_s, d), hist, wsg_s, *mw)
    return (h.reshape(batch, seq, d), cs, hs.reshape(seqs_s, rows_s, d), css,
            vns.reshape(seqs_s, rows_s, d))


def _ffn(h, p, hs, ps, fw):
    batch, seq, d = h.shape
    total = batch * seq
    total_s = hs.shape[0] * hs.shape[1]
    rows = FFN_ROWS
    assert total % rows == 0
    y, ys = pl.pallas_call(
        _ffn_kernel,
        grid=(total // rows,),
        in_specs=[pl.BlockSpec((rows, d), lambda i: (i, 0)),
                  pl.BlockSpec((rows, PLE_DIM), lambda i: (i, 0)),
                  _resident((total_s, d)), _resident((total_s, PLE_DIM))]
        + _weight_specs(fw),
        out_specs=[pl.BlockSpec((rows, d), lambda i: (i, 0)), _full((total_s, d))],
        out_shape=[jax.ShapeDtypeStruct((total, d), F32),
                   jax.ShapeDtypeStruct((total_s, d), F32)],
        scratch_shapes=_weight_scratch(fw),
        compiler_params=_params(("arbitrary",)),
        name="ffn",
    )(h.reshape(total, d), p.reshape(total, PLE_DIM), hs.reshape(total_s, d),
      ps.reshape(total_s, PLE_DIM), *fw)
    return y.reshape(batch, seq, d), ys.reshape(hs.shape)


def kernel(x_prompt, x_sample, cache_conv, p_prompt, p_sample, g_mix, w_in, g_sgv, b_sgv, w_sg, b_sg, w_a, w_dw, b_dw, g_cln, b_cln, w_b, w_out, g_ffn, w_up, w_down, g_ple, w_ple_gate, w_ple, g_final):
    depth = w_in.shape[0]
    assert depth == 1
    row = lambda t: t.reshape(1, -1)
    dec_seq = x_sample.shape[1]
    hp, hs = x_prompt, x_sample
    conv_p, conv_s, v_s = [], [], []
    for i in range(depth):
        bias = jnp.repeat(jnp.transpose(b_sg[i]), SG_GROUP_DIM, axis=1)
        tril = lambda n: jnp.tril(w_sg[i][:, :n, :n]).astype(BF16)
        mw = (row(g_mix[i]), w_in[i], row(g_sgv[i]), row(b_sgv[i]), tril(SG_CHUNK), bias, w_a[i],
              w_dw[i], row(b_dw[i]), row(g_cln[i]), row(b_cln[i]), w_b[i], w_out[i])
        fw = (row(g_ffn[i]), w_up[i], w_down[i], row(g_ple[i]), w_ple_gate[i], w_ple[i],
              row(g_final))

        h1p, cp, h1s, cs, vs = _mixer(hp, hs, cache_conv[i], tril(dec_seq), mw, SG_CHUNK)
        hp, hs = _ffn(h1p, p_prompt[i], h1s, p_sample[i], fw)
        conv_p.append(cp)
        conv_s.append(cs)
        v_s.append(vs)
    return (hp, hs, jnp.stack(conv_p), jnp.stack(conv_s), jnp.stack(v_s))
```

```python
import functools

import jax
import jax.numpy as jnp
from jax import lax
from jax.experimental import pallas as pl
from jax.experimental.pallas import tpu as pltpu

D_MODEL = 1024
SG_CHUNK = 128
SG_GROUPS = 8
SG_GROUP_DIM = D_MODEL // SG_GROUPS
CONV_WIDTH = 31
HIST = CONV_WIDTH - 1
D_FF = 4 * D_MODEL
PLE_DIM = 256
EPS = 1e-6

LANES = 128
SUBLANES = 8
HEAD_ROWS = 16
COL_BLOCKS = D_MODEL // LANES
HIST_PAD = 32
HIST_OFF = HIST_PAD - HIST
CONV_ROW_BLOCK = 128

MIXER_ROWS = 256
FFN_ROWS = 512
STAGE_ROWS = 512
STAGE_COLS = 1024
VMEM_LIMIT_BYTES = 56 * 1024 * 1024

BF16 = jnp.bfloat16
F32 = jnp.float32


def _rms(x, g):
    ms = jnp.mean(x * x, axis=-1, keepdims=True)
    return x * lax.rsqrt(ms + EPS) * g


def _ln(x, g, b):
    mu = jnp.mean(x, axis=-1, keepdims=True)
    xc = x - mu
    y = xc * lax.rsqrt(jnp.mean(xc * xc, axis=-1, keepdims=True) + EPS)
    return y * g + b


def _dot(a, b):
    return jnp.dot(a, b, preferred_element_type=F32)


def _load_weights_bf16(pairs, stage, sem):
    pieces = []
    for src, dst in pairs:
        k, n = src.shape
        assert n % STAGE_COLS == 0
        for r0 in range(0, k, STAGE_ROWS):
            for c0 in range(0, n, STAGE_COLS):
                pieces.append((src, dst, r0, min(STAGE_ROWS, k - r0), c0))

    def copy(i):
        src, _, r0, rr, c0 = pieces[i]
        return pltpu.make_async_copy(src.at[pl.ds(r0, rr), pl.ds(c0, STAGE_COLS)],
                                     stage.at[i % 2, pl.ds(0, rr), :], sem.at[i % 2])

    copy(0).start()
    for i, (_, dst, r0, rr, c0) in enumerate(pieces):
        if i + 1 < len(pieces):
            copy(i + 1).start()
        copy(i).wait()
        dst[pl.ds(r0, rr), pl.ds(c0, STAGE_COLS)] = stage[i % 2, 0:rr, :].astype(BF16)


def _conv_piece(cbuf, seq, wdw_ref, j, r0, rb):
    lanes = slice(j * LANES, (j + 1) * LANES)
    nb = rb // SUBLANES
    nq = -(-CONV_WIDTH // SUBLANES)
    acc = [None] * nb
    for s in range(SUBLANES):
        ws = [wdw_ref[q * SUBLANES + s:q * SUBLANES + s + 1, lanes]
              if q * SUBLANES + s < CONV_WIDTH else None for q in range(nq)]
        for m in range(nb + nq - 1):
            taps = [q for q in range(nq) if ws[q] is not None and 0 <= m - q < nb]
            if not taps:
                continue
            rows = cbuf[seq * COL_BLOCKS + j, pl.ds(HIST_OFF + r0 + m * SUBLANES + s, SUBLANES), :]
            for q in taps:
                term = rows * ws[q]
                acc[m - q] = term if acc[m - q] is None else acc[m - q] + term
    return jnp.concatenate(acc, axis=0)


def _depthwise_conv(cbuf, seq, rows, wdw_ref):
    return jnp.concatenate([_conv_piece(cbuf, seq, wdw_ref, j, 0, rows)
                            for j in range(COL_BLOCKS)], axis=1)


def _all_bits(x):
    bits = pltpu.bitcast(x, jnp.uint32)
    out = None
    for r in range(0, x.shape[0], SUBLANES):
        for c in range(0, x.shape[1], LANES):
            blk = bits[r:r + SUBLANES, c:c + LANES]
            out = blk if out is None else out | blk
    return out


def _after(x, token):
    if token is None:
        return x
    reps = (x.shape[0] // SUBLANES, x.shape[1] // LANES)
    return x + jnp.tile(pltpu.bitcast(token, F32), reps)


SUB_COLS = 256
FRONT_PLAN = ((0, 0, 0, 0), (4, 0, 0, 0), (2, 0, 0, 0), (2, 0, 0, 0), (4, 0, 0, 0), (4, 0, 0, 0),
              (0, 0, 0, 0))


def _front(x, n_scr, gmix, win, gsgv, bsgv, wsg, bsg, wa, chunk, emit=None):
    d = D_MODEL
    nf = _rms(x, gmix[...])
    n_scr[...] = nf.astype(BF16)
    head = nf[0:HEAD_ROWS]

    def n_lhs(token):
        if token is None:
            return n_scr[...]
        return jnp.concatenate([_after(head, token).astype(BF16), n_scr[HEAD_ROWS:, :]], axis=0)

    def dot_cols(make_lhs, w, c0, k):
        if emit is None:
            return _dot(make_lhs(None), w[:, c0:c0 + d])
        outs = []
        for i, n_pieces in enumerate(FRONT_PLAN[k]):
            token = emit(n_pieces) if n_pieces else None
            outs.append(_dot(make_lhs(token), w[:, c0 + i * SUB_COLS:c0 + (i + 1) * SUB_COLS]))
        return jnp.concatenate(outs, axis=1)

    v = dot_cols(n_lhs, win, d, 0)
    u = dot_cols(n_lhs, win, 0, 1)
    vn = _ln(v, gsgv[...], bsgv[...])
    vnb = vn.astype(BF16)
    nch = n_scr.shape[0] // chunk
    per_group = []
    for g in range(SG_GROUPS):
        lanes = slice(g * SG_GROUP_DIM, (g + 1) * SG_GROUP_DIM)
        vg = jnp.concatenate([vnb[c * chunk:(c + 1) * chunk, lanes] for c in range(nch)], axis=1)
        per_group.append(_dot(wsg[g], vg))
    bias = bsg[0:chunk, :]
    s_rows = []
    for c in range(nch):
        lanes = slice(c * SG_GROUP_DIM, (c + 1) * SG_GROUP_DIM)
        s_rows.append(jnp.concatenate([sg[:, lanes] for sg in per_group], axis=1) + bias)
    s = jnp.concatenate(s_rows, axis=0)

    ca = dot_cols(n_lhs, win, 2 * d, 2)
    us = u * s
    us_rest = us[HEAD_ROWS:].astype(BF16)

    def us_lhs(token):
        return jnp.concatenate([_after(us[0:HEAD_ROWS], token).astype(BF16), us_rest], axis=0)

    a_mix = dot_cols(us_lhs, wa, 0, 3)
    cb = dot_cols(n_lhs, win, 3 * d, 4)
    ga = dot_cols(n_lhs, win, 4 * d, 5)
    gb = dot_cols(n_lhs, win, 5 * d, 6)
    done = emit(None) if emit is not None else None
    c = ca * jax.nn.sigmoid(_after(cb, done))
    return vn, jax.nn.sigmoid(_after(ga, done)) * a_mix, jax.nn.sigmoid(_after(gb, done)), c


def _back(x, a_term, gate_b, dw, gcln, bcln, wb, wout):
    bb = jax.nn.silu(_ln(dw, gcln[...], bcln[...])).astype(BF16)
    mix = (a_term + gate_b * _dot(bb, wb[...])).astype(BF16)
    return x + _dot(mix, wout[...])


def _mixer_kernel(xc_ref, xp_ref, zero_ref, xs_ref, hist_ref, wsg_s, gmix, win_hbm, gsgv, bsgv, wsg,
                  bsg, wa_hbm, wdw, bdw, gcln, bcln, wb_hbm, wout_hbm,
                  h_ref, cs_ref, hs_ref, css_ref, vns_ref,
                  win, wa, wb, wout, stage, sem, n_scr, dw_scr, cbuf, a_ring, g_ring, cbuf_s,
                  *, rows, chunk, tiles_per_seq, tiles, seqs_s, rows_s):
    t = pl.program_id(0)
    cur = lax.rem(t, 2)
    prev = 1 - cur

    @pl.when(t == 0)
    def _():
        _load_weights_bf16([(win_hbm, win), (wa_hbm, wa), (wb_hbm, wb), (wout_hbm, wout)],
                           stage, sem)
        cbuf[...] = jnp.zeros(cbuf.shape, F32)
        a_ring[...] = jnp.zeros(a_ring.shape, F32)
        g_ring[...] = jnp.zeros(g_ring.shape, F32)

    pieces = [(j, r0) for j in range(COL_BLOCKS) for r0 in range(0, rows, CONV_ROW_BLOCK)]
    assert sum(map(sum, FRONT_PLAN)) <= len(pieces)
    todo = iter(pieces)
    last_token = [None]

    def emit(n):
        bits = None
        for j, r0 in (list(todo) if n is None else [next(todo) for _ in range(n)]):
            piece = _conv_piece(cbuf, prev, wdw, j, r0, CONV_ROW_BLOCK)
            dw_scr[r0:r0 + CONV_ROW_BLOCK, j * LANES:(j + 1) * LANES] = piece
            bits = _all_bits(piece) if bits is None else bits | _all_bits(piece)
        if bits is not None:
            last_token[0] = bits & zero_ref[...]
            return last_token[0]
        return last_token[0] if n is None else None

    _, a_term, gate_b, c = _front(xc_ref[...], n_scr, gmix, win, gsgv, bsgv, wsg, bsg, wa, chunk,
                                  emit=emit)
    dw = dw_scr[...] + bdw[...]
    a_ring[cur] = a_term
    g_ring[cur] = gate_b
    first = lax.rem(t, tiles_per_seq) == 0
    for j in range(COL_BLOCKS):
        lanes = slice(j * LANES, (j + 1) * LANES)
        tail = cbuf[prev * COL_BLOCKS + j, rows:rows + HIST_PAD, :]
        cbuf[cur * COL_BLOCKS + j, 0:HIST_PAD, :] = jnp.where(first, 0.0, tail)
        cbuf[cur * COL_BLOCKS + j, HIST_PAD:HIST_PAD + rows, :] = c[:, lanes]

    for j in range(COL_BLOCKS):
        lanes = slice(j * LANES, (j + 1) * LANES)
        cs_ref[:, lanes] = cbuf[prev * COL_BLOCKS + j, rows + HIST_OFF:rows + HIST_PAD, :]
    h_ref[...] = _back(xp_ref[...], a_ring[prev], g_ring[prev], dw, gcln, bcln, wb, wout)

    @pl.when(t == tiles)
    def _():
        x = xs_ref[...]
        vn, a_s, g_s, c_s = _front(x, n_scr, gmix, win, gsgv, bsgv, wsg_s, bsg, wa, rows_s)
        vns_ref[...] = vn
        for q in range(seqs_s):
            for j in range(COL_BLOCKS):
                lanes = slice(j * LANES, (j + 1) * LANES)
                cbuf_s[q * COL_BLOCKS + j, HIST_OFF:HIST_PAD, :] = hist_ref[q, :, lanes]
                cbuf_s[q * COL_BLOCKS + j, HIST_PAD:HIST_PAD + rows_s, :] = (
                    c_s[q * rows_s:(q + 1) * rows_s, lanes])
        dw_s = jnp.concatenate([_depthwise_conv(cbuf_s, q, rows_s, wdw) for q in range(seqs_s)],
                               axis=0)
        for q in range(seqs_s):
            for j in range(COL_BLOCKS):
                lanes = slice(j * LANES, (j + 1) * LANES)
                css_ref[q, :, lanes] = cbuf_s[q * COL_BLOCKS + j,
                                              rows_s + HIST_OFF:rows_s + HIST_PAD, :]
        hs_ref[...] = _back(x, a_s, g_s, dw_s + bdw[...], gcln, bcln, wb, wout)


def _ffn_rows(h, p, gffn, wup, wdown, gple, wgate, wple, gfinal):
    f = _dot(_rms(h, gffn[...]).astype(BF16), wup[...])
    r = jnp.square(jnp.maximum(f, 0.0)).astype(BF16)
    h = h + _dot(r, wdown[...])
    gate = jax.nn.sigmoid(_dot(_rms(h, gple[...]).astype(BF16), wgate[...]))
    e = _dot(p.astype(BF16), wple[...])
    h = h + gate * e
    return _rms(h, gfinal[...])


def _ffn_kernel(h_ref, p_ref, hs_ref, ps_ref, gffn, wup_hbm, wdown_hbm, gple, wgate_hbm, wple_hbm,
                gfinal, y_ref, ys_ref, wup, wdown, wgate, wple, stage, sem):
    @pl.when(pl.program_id(0) == 0)
    def _():
        _load_weights_bf16([(wup_hbm, wup), (wdown_hbm, wdown), (wgate_hbm, wgate),
                            (wple_hbm, wple)], stage, sem)

    y_ref[...] = _ffn_rows(h_ref[...], p_ref[...], gffn, wup, wdown, gple, wgate, wple, gfinal)

    @pl.when(pl.program_id(0) == pl.num_programs(0) - 1)
    def _():
        ys_ref[...] = _ffn_rows(hs_ref[...], ps_ref[...], gffn, wup, wdown, gple, wgate, wple,
                                gfinal)


def _resident(shape):
    zeros = (0,) * len(shape)
    return pl.BlockSpec(shape, lambda *_: zeros, pipeline_mode=pl.Buffered(1))


def _full(shape):
    zeros = (0,) * len(shape)
    return pl.BlockSpec(shape, lambda *_: zeros)


def _weight_specs(weights):
    return [pl.BlockSpec(memory_space=pl.ANY) if _is_matrix(w) else _resident(w.shape)
            for w in weights]


def _is_matrix(w):
    return w.ndim == 2 and w.dtype == F32 and w.shape[0] >= PLE_DIM and w.shape[1] >= STAGE_COLS


def _weight_scratch(weights):
    return ([pltpu.VMEM(w.shape, BF16) for w in weights if _is_matrix(w)]
            + [pltpu.VMEM((2, STAGE_ROWS, STAGE_COLS), F32), pltpu.SemaphoreType.DMA((2,))])


def _params(semantics):
    return pltpu.CompilerParams(dimension_semantics=semantics, vmem_limit_bytes=VMEM_LIMIT_BYTES)


def _mixer(x, xs, hist, wsg_s, mw, chunk):
    batch, seq, d = x.shape
    seqs_s, rows_s, _ = xs.shape
    rows = MIXER_ROWS
    total_s = seqs_s * rows_s
    assert seq % rows == 0 and rows % chunk == 0 and total_s == rows
    tiles_per_seq = seq // rows
    tiles = batch * tiles_per_seq
    kern = functools.partial(_mixer_kernel, rows=rows, chunk=chunk, tiles_per_seq=tiles_per_seq,
                             tiles=tiles, seqs_s=seqs_s, rows_s=rows_s)
    cur_tile = lambda t: (jnp.minimum(t, tiles - 1), 0)
    prev_tile = lambda t: (jnp.maximum(t - 1, 0), 0)
    x2 = x.reshape(tiles * rows, d)
    h, cs, hs, css, vns = pl.pallas_call(
        kern,
        grid=(tiles + 1,),
        in_specs=[pl.BlockSpec((rows, d), cur_tile), pl.BlockSpec((rows, d), prev_tile),
                  _resident((SUBLANES, LANES)), _resident((total_s, d)), _resident(hist.shape),
                  _resident(wsg_s.shape)]
        + _weight_specs(mw),
        out_specs=[pl.BlockSpec((rows, d), prev_tile),
                   pl.BlockSpec((None, HIST, d),
                                lambda t: (jnp.maximum(t - 1, 0) // tiles_per_seq, 0, 0)),
                   _full((total_s, d)), _full((seqs_s, HIST, d)), _full((total_s, d))],
        out_shape=[jax.ShapeDtypeStruct((tiles * rows, d), F32),
                   jax.ShapeDtypeStruct((batch, HIST, d), F32),
                   jax.ShapeDtypeStruct((total_s, d), F32),
                   jax.ShapeDtypeStruct((seqs_s, HIST, d), F32),
                   jax.ShapeDtypeStruct((total_s, d), F32)],
        scratch_shapes=_weight_scratch(mw)
        + [pltpu.VMEM((rows, d), BF16),
           pltpu.VMEM((rows, d), F32),
           pltpu.VMEM((2 * COL_BLOCKS, HIST_PAD + rows, LANES), F32),
           pltpu.VMEM((2, rows, d), F32),
           pltpu.VMEM((2, rows, d), F32),
           pltpu.VMEM((seqs_s * COL_BLOCKS, HIST_PAD + rows_s, LANES), F32)],
        compiler_params=_params(("arbitrary",)),
        name="mixer",
    )(x2, x2, jnp.zeros((SUBLANES, LANES), jnp.uint32), xs.reshape(total_s, d), hist, wsg_s, *mw)
    return (h.reshape(batch, seq, d), cs, hs.reshape(seqs_s, rows_s, d), css,
            vns.reshape(seqs_s, rows_s, d))


def _ffn(h, p, hs, ps, fw):
    batch, seq, d = h.shape
    total = batch * seq
    total_s = hs.shape[0] * hs.shape[1]
    rows = FFN_ROWS
    assert total % rows == 0
    y, ys = pl.pallas_call(
        _ffn_kernel,
        grid=(total // rows,),
        in_specs=[pl.BlockSpec((rows, d), lambda i: (i, 0)),
                  pl.BlockSpec((rows, PLE_DIM), lambda i: (i, 0)),
                  _resident((total_s, d)), _resident((total_s, PLE_DIM))]
        + _weight_specs(fw),
        out_specs=[pl.BlockSpec((rows, d), lambda i: (i, 0)), _full((total_s, d))],
        out_shape=[jax.ShapeDtypeStruct((total, d), F32),
                   jax.ShapeDtypeStruct((total_s, d), F32)],
        scratch_shapes=_weight_scratch(fw),
        compiler_params=_params(("arbitrary",)),
        name="ffn",
    )(h.reshape(total, d), p.reshape(total, PLE_DIM), hs.reshape(total_s, d),
      ps.reshape(total_s, PLE_DIM), *fw)
    return y.reshape(batch, seq, d), ys.reshape(hs.shape)


def kernel(x_prompt, x_sample, cache_conv, p_prompt, p_sample, g_mix, w_in, g_sgv, b_sgv, w_sg, b_sg, w_a, w_dw, b_dw, g_cln, b_cln, w_b, w_out, g_ffn, w_up, w_down, g_ple, w_ple_gate, w_ple, g_final):
    depth = w_in.shape[0]
    assert depth == 1
    row = lambda t: t.reshape(1, -1)
    dec_seq = x_sample.shape[1]
    hp, hs = x_prompt, x_sample
    conv_p, conv_s, v_s = [], [], []
    for i in range(depth):
        bias = jnp.repeat(jnp.transpose(b_sg[i]), SG_GROUP_DIM, axis=1)
        tril = lambda n: jnp.tril(w_sg[i][:, :n, :n]).astype(BF16)
        mw = (row(g_mix[i]), w_in[i], row(g_sgv[i]), row(b_sgv[i]), tril(SG_CHUNK), bias, w_a[i],
              w_dw[i], row(b_dw[i]), row(g_cln[i]), row(b_cln[i]), w_b[i], w_out[i])
        fw = (row(g_ffn[i]), w_up[i], w_down[i], row(g_ple[i]), w_ple_gate[i], w_ple[i],
              row(g_final))

        h1p, cp, h1s, cs, vs = _mixer(hp, hs, cache_conv[i], tril(dec_seq), mw, SG_CHUNK)
        hp, hs = _ffn(h1p, p_prompt[i], h1s, p_sample[i], fw)
        conv_p.append(cp)
        conv_s.append(cs)
        v_s.append(vs)
    return (hp, hs, jnp.stack(conv_p), jnp.stack(conv_s), jnp.stack(v_s))
```

```python
import functools

import jax
import jax.numpy as jnp
from jax import lax
from jax.experimental import pallas as pl
from jax.experimental.pallas import tpu as pltpu

D_MODEL = 1024
SG_CHUNK = 128
SG_GROUPS = 8
SG_GROUP_DIM = D_MODEL // SG_GROUPS
CONV_WIDTH = 31
HIST = CONV_WIDTH - 1
D_FF = 4 * D_MODEL
PLE_DIM = 256
EPS = 1e-6

LANES = 128
SUBLANES = 8
HEAD_ROWS = 16
COL_BLOCKS = D_MODEL // LANES
HIST_PAD = 32
HIST_OFF = HIST_PAD - HIST
CONV_ROW_BLOCK = 128

MIXER_ROWS = 256
FFN_ROWS = 512
STAGE_ROWS = 512
STAGE_COLS = 1024
VMEM_LIMIT_BYTES = 56 * 1024 * 1024

BF16 = jnp.bfloat16
F32 = jnp.float32


def _rms(x, g):
    ms = jnp.mean(x * x, axis=-1, keepdims=True)
    return x * lax.rsqrt(ms + EPS) * g


def _ln(x, g, b):
    mu = jnp.mean(x, axis=-1, keepdims=True)
    xc = x - mu
    y = xc * lax.rsqrt(jnp.mean(xc * xc, axis=-1, keepdims=True) + EPS)
    return y * g + b


def _dot(a, b):
    return jnp.dot(a, b, preferred_element_type=F32)


def _load_weights_bf16(pairs, stage, sem):
    pieces = []
    for src, dst in pairs:
        k, n = src.shape
        assert n % STAGE_COLS == 0
        for r0 in range(0, k, STAGE_ROWS):
            for c0 in range(0, n, STAGE_COLS):
                pieces.append((src, dst, r0, min(STAGE_ROWS, k - r0), c0))

    def copy(i):
        src, _, r0, rr, c0 = pieces[i]
        return pltpu.make_async_copy(src.at[pl.ds(r0, rr), pl.ds(c0, STAGE_COLS)],
                                     stage.at[i % 2, pl.ds(0, rr), :], sem.at[i % 2])

    copy(0).start()
    for i, (_, dst, r0, rr, c0) in enumerate(pieces):
        if i + 1 < len(pieces):
            copy(i + 1).start()
        copy(i).wait()
        dst[pl.ds(r0, rr), pl.ds(c0, STAGE_COLS)] = stage[i % 2, 0:rr, :].astype(BF16)


def _conv_piece(cbuf, seq, wdw_ref, j, r0, rb):
    lanes = slice(j * LANES, (j + 1) * LANES)
    nb = rb // SUBLANES
    nq = -(-CONV_WIDTH // SUBLANES)
    acc = [None] * nb
    for s in range(SUBLANES):
        ws = [wdw_ref[q * SUBLANES + s:q * SUBLANES + s + 1, lanes]
              if q * SUBLANES + s < CONV_WIDTH else None for q in range(nq)]
        for m in range(nb + nq - 1):
            taps = [q for q in range(nq) if ws[q] is not None and 0 <= m - q < nb]
            if not taps:
                continue
            rows = cbuf[seq * COL_BLOCKS + j, pl.ds(HIST_OFF + r0 + m * SUBLANES + s, SUBLANES), :]
            for q in taps:
                term = rows * ws[q]
                acc[m - q] = term if acc[m - q] is None else acc[m - q] + term
    return jnp.concatenate(acc, axis=0)


def _depthwise_conv(cbuf, seq, rows, wdw_ref):
    return jnp.concatenate([_conv_piece(cbuf, seq, wdw_ref, j, 0, rows)
                            for j in range(COL_BLOCKS)], axis=1)


def _all_bits(x):
    bits = pltpu.bitcast(x, jnp.uint32)
    out = None
    for r in range(0, x.shape[0], SUBLANES):
        for c in range(0, x.shape[1], LANES):
            blk = bits[r:r + SUBLANES, c:c + LANES]
            out = blk if out is None else out | blk
    return out


def _after(x, token):
    if token is None:
        return x
    reps = (x.shape[0] // SUBLANES, x.shape[1] // LANES)
    return x + jnp.tile(pltpu.bitcast(token, F32), reps)


FRONT_GROUPS = 7
GROUP_PIECES = (4, 2, 2, 4, 4, 0, 0)


def _front(x, n_scr, gmix, win, gsgv, bsgv, wsg, bsg, wa, chunk, emit_group=None):
    if emit_group is None:
        emit_group = lambda k: None
    d = D_MODEL
    nf = _rms(x, gmix[...])
    n_scr[...] = nf.astype(BF16)
    head = nf[0:HEAD_ROWS]

    def lhs(token):
        if token is None:
            return n_scr[...]
        return jnp.concatenate([_after(head, token).astype(BF16), n_scr[HEAD_ROWS:, :]], axis=0)

    def cast_after(y, token):
        if token is None:
            return y.astype(BF16)
        return jnp.concatenate([_after(y[0:HEAD_ROWS], token).astype(BF16),
                                y[HEAD_ROWS:].astype(BF16)], axis=0)

    v = _dot(n_scr[...], win[:, d:2 * d])
    u = _dot(lhs(emit_group(0)), win[:, 0:d])
    vn = _ln(v, gsgv[...], bsgv[...])
    vnb = vn.astype(BF16)
    nch = x.shape[0] // chunk
    per_group = []
    for g in range(SG_GROUPS):
        lanes = slice(g * SG_GROUP_DIM, (g + 1) * SG_GROUP_DIM)
        vg = jnp.concatenate([vnb[c * chunk:(c + 1) * chunk, lanes] for c in range(nch)], axis=1)
        per_group.append(_dot(wsg[g], vg))
    bias = bsg[0:chunk, :]
    s_rows = []
    for c in range(nch):
        lanes = slice(c * SG_GROUP_DIM, (c + 1) * SG_GROUP_DIM)
        s_rows.append(jnp.concatenate([sg[:, lanes] for sg in per_group], axis=1) + bias)
    s = jnp.concatenate(s_rows, axis=0)

    ca = _dot(lhs(emit_group(1)), win[:, 2 * d:3 * d])
    a_mix = _dot(cast_after(u * s, emit_group(2)), wa[...])
    cb = _dot(lhs(emit_group(3)), win[:, 3 * d:4 * d])
    ga = _dot(lhs(emit_group(4)), win[:, 4 * d:5 * d])
    c = ca * jax.nn.sigmoid(cb)
    gb = _dot(lhs(emit_group(5)), win[:, 5 * d:6 * d])
    emit_group(6)
    return vn, jax.nn.sigmoid(ga) * a_mix, jax.nn.sigmoid(gb), c


def _back(x, a_term, gate_b, dw, gcln, bcln, wb, wout):
    bb = jax.nn.silu(_ln(dw, gcln[...], bcln[...])).astype(BF16)
    mix = (a_term + gate_b * _dot(bb, wb[...])).astype(BF16)
    return x + _dot(mix, wout[...])


def _mixer_kernel(xc_ref, xp_ref, zero_ref, xs_ref, hist_ref, wsg_s, gmix, win_hbm, gsgv, bsgv, wsg,
                  bsg, wa_hbm, wdw, bdw, gcln, bcln, wb_hbm, wout_hbm,
                  h_ref, cs_ref, hs_ref, css_ref, vns_ref,
                  win, wa, wb, wout, stage, sem, n_scr, dw_scr, cbuf, a_ring, g_ring, cbuf_s,
                  *, rows, chunk, tiles_per_seq, tiles, seqs_s, rows_s):
    t = pl.program_id(0)
    cur = lax.rem(t, 2)
    prev = 1 - cur

    @pl.when(t == 0)
    def _():
        _load_weights_bf16([(win_hbm, win), (wa_hbm, wa), (wb_hbm, wb), (wout_hbm, wout)],
                           stage, sem)
        cbuf[...] = jnp.zeros(cbuf.shape, F32)
        a_ring[...] = jnp.zeros(a_ring.shape, F32)
        g_ring[...] = jnp.zeros(g_ring.shape, F32)

    pieces = [(j, r0) for j in range(COL_BLOCKS) for r0 in range(0, rows, CONV_ROW_BLOCK)]
    assert sum(GROUP_PIECES) == len(pieces)
    bounds = [sum(GROUP_PIECES[:g]) for g in range(FRONT_GROUPS + 1)]

    def emit_group(g):
        bits = None
        for j, r0 in pieces[bounds[g]:bounds[g + 1]]:
            piece = _conv_piece(cbuf, prev, wdw, j, r0, CONV_ROW_BLOCK)
            dw_scr[r0:r0 + CONV_ROW_BLOCK, j * LANES:(j + 1) * LANES] = piece
            bits = _all_bits(piece) if bits is None else bits | _all_bits(piece)
        return None if bits is None else bits & zero_ref[...]

    _, a_term, gate_b, c = _front(xc_ref[...], n_scr, gmix, win, gsgv, bsgv, wsg, bsg, wa, chunk,
                                  emit_group=emit_group)
    dw = dw_scr[...] + bdw[...]
    a_ring[cur] = a_term
    g_ring[cur] = gate_b
    first = lax.rem(t, tiles_per_seq) == 0
    for j in range(COL_BLOCKS):
        lanes = slice(j * LANES, (j + 1) * LANES)
        tail = cbuf[prev * COL_BLOCKS + j, rows:rows + HIST_PAD, :]
        cbuf[cur * COL_BLOCKS + j, 0:HIST_PAD, :] = jnp.where(first, 0.0, tail)
        cbuf[cur * COL_BLOCKS + j, HIST_PAD:HIST_PAD + rows, :] = c[:, lanes]

    for j in range(COL_BLOCKS):
        lanes = slice(j * LANES, (j + 1) * LANES)
        cs_ref[:, lanes] = cbuf[prev * COL_BLOCKS + j, rows + HIST_OFF:rows + HIST_PAD, :]
    h_ref[...] = _back(xp_ref[...], a_ring[prev], g_ring[prev], dw, gcln, bcln, wb, wout)

    @pl.when(t == tiles)
    def _():
        x = xs_ref[...]
        vn, a_s, g_s, c_s = _front(x, n_scr, gmix, win, gsgv, bsgv, wsg_s, bsg, wa, rows_s)
        vns_ref[...] = vn
        for q in range(seqs_s):
            for j in range(COL_BLOCKS):
                lanes = slice(j * LANES, (j + 1) * LANES)
                cbuf_s[q * COL_BLOCKS + j, HIST_OFF:HIST_PAD, :] = hist_ref[q, :, lanes]
                cbuf_s[q * COL_BLOCKS + j, HIST_PAD:HIST_PAD + rows_s, :] = (
                    c_s[q * rows_s:(q + 1) * rows_s, lanes])
        dw_s = jnp.concatenate([_depthwise_conv(cbuf_s, q, rows_s, wdw) for q in range(seqs_s)],
                               axis=0)
        for q in range(seqs_s):
            for j in range(COL_BLOCKS):
                lanes = slice(j * LANES, (j + 1) * LANES)
                css_ref[q, :, lanes] = cbuf_s[q * COL_BLOCKS + j,
                                              rows_s + HIST_OFF:rows_s + HIST_PAD, :]
        hs_ref[...] = _back(x, a_s, g_s, dw_s + bdw[...], gcln, bcln, wb, wout)


def _ffn_rows(h, p, r_scr, gffn, wup, wdown, gple, wgate, wple, gfinal):
    rows = h.shape[0]
    f = _dot(_rms(h, gffn[...]).astype(BF16), wup[...])
    r_scr[0:rows, :] = jnp.square(jnp.maximum(f, 0.0)).astype(BF16)
    h = h + _dot(r_scr[0:rows, :], wdown[...])
    gate = jax.nn.sigmoid(_dot(_rms(h, gple[...]).astype(BF16), wgate[...]))
    e = _dot(p.astype(BF16), wple[...])
    h = h + gate * e
    return _rms(h, gfinal[...])


def _ffn_kernel(h_ref, p_ref, hs_ref, ps_ref, gffn, wup_hbm, wdown_hbm, gple, wgate_hbm, wple_hbm,
                gfinal, y_ref, ys_ref, wup, wdown, wgate, wple, stage, sem, r_scr):
    @pl.when(pl.program_id(0) == 0)
    def _():
        _load_weights_bf16([(wup_hbm, wup), (wdown_hbm, wdown), (wgate_hbm, wgate),
                            (wple_hbm, wple)], stage, sem)

    y_ref[...] = _ffn_rows(h_ref[...], p_ref[...], r_scr, gffn, wup, wdown, gple, wgate, wple,
                           gfinal)

    @pl.when(pl.program_id(0) == pl.num_programs(0) - 1)
    def _():
        ys_ref[...] = _ffn_rows(hs_ref[...], ps_ref[...], r_scr, gffn, wup, wdown, gple, wgate,
                                wple, gfinal)


def _resident(shape):
    zeros = (0,) * len(shape)
    return pl.BlockSpec(shape, lambda *_: zeros, pipeline_mode=pl.Buffered(1))


def _full(shape):
    zeros = (0,) * len(shape)
    return pl.BlockSpec(shape, lambda *_: zeros)


def _weight_specs(weights):
    return [pl.BlockSpec(memory_space=pl.ANY) if _is_matrix(w) else _resident(w.shape)
            for w in weights]


def _is_matrix(w):
    return w.ndim == 2 and w.dtype == F32 and w.shape[0] >= PLE_DIM and w.shape[1] >= STAGE_COLS


def _weight_scratch(weights):
    return ([pltpu.VMEM(w.shape, BF16) for w in weights if _is_matrix(w)]
            + [pltpu.VMEM((2, STAGE_ROWS, STAGE_COLS), F32), pltpu.SemaphoreType.DMA((2,))])


def _params(semantics):
    return pltpu.CompilerParams(dimension_semantics=semantics, vmem_limit_bytes=VMEM_LIMIT_BYTES)


def _mixer(x, xs, hist, wsg_s, mw, chunk):
    batch, seq, d = x.shape
    seqs_s, rows_s, _ = xs.shape
    rows = MIXER_ROWS
    total_s = seqs_s * rows_s
    assert seq % rows == 0 and rows % chunk == 0 and total_s == rows
    tiles_per_seq = seq // rows
    tiles = batch * tiles_per_seq
    kern = functools.partial(_mixer_kernel, rows=rows, chunk=chunk, tiles_per_seq=tiles_per_seq,
                             tiles=tiles, seqs_s=seqs_s, rows_s=rows_s)
    cur_tile = lambda t: (jnp.minimum(t, tiles - 1), 0)
    prev_tile = lambda t: (jnp.maximum(t - 1, 0), 0)
    x2 = x.reshape(tiles * rows, d)
    h, cs, hs, css, vns = pl.pallas_call(
        kern,
        grid=(tiles + 1,),
        in_specs=[pl.BlockSpec((rows, d), cur_tile), pl.BlockSpec((rows, d), prev_tile),
                  _resident((SUBLANES, LANES)), _resident((total_s, d)), _resident(hist.shape),
                  _resident(wsg_s.shape)]
        + _weight_specs(mw),
        out_specs=[pl.BlockSpec((rows, d), prev_tile),
                   pl.BlockSpec((None, HIST, d),
                                lambda t: (jnp.maximum(t - 1, 0) // tiles_per_seq, 0, 0)),
                   _full((total_s, d)), _full((seqs_s, HIST, d)), _full((total_s, d))],
        out_shape=[jax.ShapeDtypeStruct((tiles * rows, d), F32),
                   jax.ShapeDtypeStruct((batch, HIST, d), F32),
                   jax.ShapeDtypeStruct((total_s, d), F32),
                   jax.ShapeDtypeStruct((seqs_s, HIST, d), F32),
                   jax.ShapeDtypeStruct((total_s, d), F32)],
        scratch_shapes=_weight_scratch(mw)
        + [pltpu.VMEM((rows, d), BF16),
           pltpu.VMEM((rows, d), F32),
           pltpu.VMEM((2 * COL_BLOCKS, HIST_PAD + rows, LANES), F32),
           pltpu.VMEM((2, rows, d), F32),
           pltpu.VMEM((2, rows, d), F32),
           pltpu.VMEM((seqs_s * COL_BLOCKS, HIST_PAD + rows_s, LANES), F32)],
        compiler_params=_params(("arbitrary",)),
        name="mixer",
    )(x2, x2, jnp.zeros((SUBLANES, LANES), jnp.uint32), xs.reshape(total_s, d), hist, wsg_s, *mw)
    return (h.reshape(batch, seq, d), cs, hs.reshape(seqs_s, rows_s, d), css,
            vns.reshape(seqs_s, rows_s, d))


def _ffn(h, p, hs, ps, fw):
    batch, seq, d = h.shape
    total = batch * seq
    total_s = hs.shape[0] * hs.shape[1]
    rows = FFN_ROWS
    assert total % rows == 0
    y, ys = pl.pallas_call(
        _ffn_kernel,
        grid=(total // rows,),
        in_specs=[pl.BlockSpec((rows, d), lambda i: (i, 0)),
                  pl.BlockSpec((rows, PLE_DIM), lambda i: (i, 0)),
                  _resident((total_s, d)), _resident((total_s, PLE_DIM))]
        + _weight_specs(fw),
        out_specs=[pl.BlockSpec((rows, d), lambda i: (i, 0)), _full((total_s, d))],
        out_shape=[jax.ShapeDtypeStruct((total, d), F32),
                   jax.ShapeDtypeStruct((total_s, d), F32)],
        scratch_shapes=_weight_scratch(fw) + [pltpu.VMEM((rows, D_FF), BF16)],
        compiler_params=_params(("arbitrary",)),
        name="ffn",
    )(h.reshape(total, d), p.reshape(total, PLE_DIM), hs.reshape(total_s, d),
      ps.reshape(total_s, PLE_DIM), *fw)
    return y.reshape(batch, seq, d), ys.reshape(hs.shape)


def kernel(x_prompt, x_sample, cache_conv, p_prompt, p_sample, g_mix, w_in, g_sgv, b_sgv, w_sg, b_sg, w_a, w_dw, b_dw, g_cln, b_cln, w_b, w_out, g_ffn, w_up, w_down, g_ple, w_ple_gate, w_ple, g_final):
    depth = w_in.shape[0]
    assert depth == 1
    row = lambda t: t.reshape(1, -1)
    dec_seq = x_sample.shape[1]
    hp, hs = x_prompt, x_sample
    conv_p, conv_s, v_s = [], [], []
    for i in range(depth):
        bias = jnp.repeat(jnp.transpose(b_sg[i]), SG_GROUP_DIM, axis=1)
        tril = lambda n: jnp.tril(w_sg[i][:, :n, :n]).astype(BF16)
        mw = (row(g_mix[i]), w_in[i], row(g_sgv[i]), row(b_sgv[i]), tril(SG_CHUNK), bias, w_a[i],
              w_dw[i], row(b_dw[i]), row(g_cln[i]), row(b_cln[i]), w_b[i], w_out[i])
        fw = (row(g_ffn[i]), w_up[i], w_down[i], row(g_ple[i]), w_ple_gate[i], w_ple[i],
              row(g_final))

        h1p, cp, h1s, cs, vs = _mixer(hp, hs, cache_conv[i], tril(dec_seq), mw, SG_CHUNK)
        hp, hs = _ffn(h1p, p_prompt[i], h1s, p_sample[i], fw)
        conv_p.append(cp)
        conv_s.append(cs)
        v_s.append(vs)
    return (hp, hs, jnp.stack(conv_p), jnp.stack(conv_s), jnp.stack(v_s))
```

```python
import functools

import jax
import jax.numpy as jnp
from jax import lax
from jax.experimental import pallas as pl
from jax.experimental.pallas import tpu as pltpu

D_MODEL = 1024
SG_CHUNK = 128
SG_GROUPS = 8
SG_GROUP_DIM = D_MODEL // SG_GROUPS
CONV_WIDTH = 31
HIST = CONV_WIDTH - 1
D_FF = 4 * D_MODEL
PLE_DIM = 256
EPS = 1e-6

LANES = 128
SUBLANES = 8
HEAD_ROWS = 16
COL_BLOCKS = D_MODEL // LANES
HIST_PAD = 32
HIST_OFF = HIST_PAD - HIST
CONV_ROW_BLOCK = 64

MIXER_ROWS = 256
FFN_ROWS = 512
STAGE_ROWS = 512
STAGE_COLS = 1024
VMEM_LIMIT_BYTES = 56 * 1024 * 1024

BF16 = jnp.bfloat16
F32 = jnp.float32


def _rms(x, g):
    ms = jnp.mean(x * x, axis=-1, keepdims=True)
    return x * lax.rsqrt(ms + EPS) * g


def _ln(x, g, b):
    mu = jnp.mean(x, axis=-1, keepdims=True)
    xc = x - mu
    y = xc * lax.rsqrt(jnp.mean(xc * xc, axis=-1, keepdims=True) + EPS)
    return y * g + b


def _dot(a, b):
    return jnp.dot(a, b, preferred_element_type=F32)


def _load_weights_bf16(pairs, stage, sem):
    pieces = []
    for src, dst in pairs:
        k, n = src.shape
        assert n % STAGE_COLS == 0
        for r0 in range(0, k, STAGE_ROWS):
            for c0 in range(0, n, STAGE_COLS):
                pieces.append((src, dst, r0, min(STAGE_ROWS, k - r0), c0))

    def copy(i):
        src, _, r0, rr, c0 = pieces[i]
        return pltpu.make_async_copy(src.at[pl.ds(r0, rr), pl.ds(c0, STAGE_COLS)],
                                     stage.at[i % 2, pl.ds(0, rr), :], sem.at[i % 2])

    copy(0).start()
    for i, (_, dst, r0, rr, c0) in enumerate(pieces):
        if i + 1 < len(pieces):
            copy(i + 1).start()
        copy(i).wait()
        dst[pl.ds(r0, rr), pl.ds(c0, STAGE_COLS)] = stage[i % 2, 0:rr, :].astype(BF16)


def _conv_piece(cbuf, seq, wdw_ref, j, r0, rb):
    lanes = slice(j * LANES, (j + 1) * LANES)
    nb = rb // SUBLANES
    nq = -(-CONV_WIDTH // SUBLANES)
    acc = [None] * nb
    for s in range(SUBLANES):
        ws = [wdw_ref[q * SUBLANES + s:q * SUBLANES + s + 1, lanes]
              if q * SUBLANES + s < CONV_WIDTH else None for q in range(nq)]
        for m in range(nb + nq - 1):
            taps = [q for q in range(nq) if ws[q] is not None and 0 <= m - q < nb]
            if not taps:
                continue
            rows = cbuf[seq * COL_BLOCKS + j, pl.ds(HIST_OFF + r0 + m * SUBLANES + s, SUBLANES), :]
            for q in taps:
                term = rows * ws[q]
                acc[m - q] = term if acc[m - q] is None else acc[m - q] + term
    return jnp.concatenate(acc, axis=0)


def _depthwise_conv(cbuf, seq, rows, wdw_ref):
    return jnp.concatenate([_conv_piece(cbuf, seq, wdw_ref, j, 0, rows)
                            for j in range(COL_BLOCKS)], axis=1)


def _all_bits(x):
    bits = pltpu.bitcast(x, jnp.uint32)
    out = None
    for r in range(0, x.shape[0], SUBLANES):
        for c in range(0, x.shape[1], LANES):
            blk = bits[r:r + SUBLANES, c:c + LANES]
            out = blk if out is None else out | blk
    return out


def _after(x, token):
    if token is None:
        return x
    reps = (x.shape[0] // SUBLANES, x.shape[1] // LANES)
    return x + jnp.tile(pltpu.bitcast(token, F32), reps)


FRONT_GROUPS = 7
GROUP_PIECES = (8, 4, 4, 8, 8, 0, 0)


def _front(x, n_scr, gmix, win, gsgv, bsgv, wsg, bsg, wa, chunk, emit_group=None):
    if emit_group is None:
        emit_group = lambda k: None
    d = D_MODEL
    nf = _rms(x, gmix[...])
    n_scr[...] = nf.astype(BF16)
    head = nf[0:HEAD_ROWS]

    def lhs(token):
        if token is None:
            return n_scr[...]
        return jnp.concatenate([_after(head, token).astype(BF16), n_scr[HEAD_ROWS:, :]], axis=0)

    def cast_after(y, token):
        if token is None:
            return y.astype(BF16)
        return jnp.concatenate([_after(y[0:HEAD_ROWS], token).astype(BF16),
                                y[HEAD_ROWS:].astype(BF16)], axis=0)

    v = _dot(n_scr[...], win[:, d:2 * d])
    u = _dot(lhs(emit_group(0)), win[:, 0:d])
    vn = _ln(v, gsgv[...], bsgv[...])
    vnb = vn.astype(BF16)
    nch = x.shape[0] // chunk
    per_group = []
    for g in range(SG_GROUPS):
        lanes = slice(g * SG_GROUP_DIM, (g + 1) * SG_GROUP_DIM)
        vg = jnp.concatenate([vnb[c * chunk:(c + 1) * chunk, lanes] for c in range(nch)], axis=1)
        per_group.append(_dot(wsg[g], vg))
    bias = bsg[0:chunk, :]
    s_rows = []
    for c in range(nch):
        lanes = slice(c * SG_GROUP_DIM, (c + 1) * SG_GROUP_DIM)
        s_rows.append(jnp.concatenate([sg[:, lanes] for sg in per_group], axis=1) + bias)
    s = jnp.concatenate(s_rows, axis=0)

    ca = _dot(lhs(emit_group(1)), win[:, 2 * d:3 * d])
    a_mix = _dot(cast_after(u * s, emit_group(2)), wa[...])
    cb = _dot(lhs(emit_group(3)), win[:, 3 * d:4 * d])
    ga = _dot(lhs(emit_group(4)), win[:, 4 * d:5 * d])
    c = ca * jax.nn.sigmoid(cb)
    gb = _dot(lhs(emit_group(5)), win[:, 5 * d:6 * d])
    emit_group(6)
    return vn, jax.nn.sigmoid(ga) * a_mix, jax.nn.sigmoid(gb), c


def _back(x, a_term, gate_b, dw, gcln, bcln, wb, wout):
    bb = jax.nn.silu(_ln(dw, gcln[...], bcln[...])).astype(BF16)
    mix = (a_term + gate_b * _dot(bb, wb[...])).astype(BF16)
    return x + _dot(mix, wout[...])


def _mixer_kernel(xc_ref, xp_ref, zero_ref, xs_ref, hist_ref, wsg_s, gmix, win_hbm, gsgv, bsgv, wsg,
                  bsg, wa_hbm, wdw, bdw, gcln, bcln, wb_hbm, wout_hbm,
                  h_ref, cs_ref, hs_ref, css_ref, vns_ref,
                  win, wa, wb, wout, stage, sem, n_scr, dw_scr, cbuf, a_ring, g_ring, cbuf_s,
                  *, rows, chunk, tiles_per_seq, tiles, seqs_s, rows_s):
    t = pl.program_id(0)
    cur = lax.rem(t, 2)
    prev = 1 - cur

    @pl.when(t == 0)
    def _():
        _load_weights_bf16([(win_hbm, win), (wa_hbm, wa), (wb_hbm, wb), (wout_hbm, wout)],
                           stage, sem)
        cbuf[...] = jnp.zeros(cbuf.shape, F32)
        a_ring[...] = jnp.zeros(a_ring.shape, F32)
        g_ring[...] = jnp.zeros(g_ring.shape, F32)

    pieces = [(j, r0) for j in range(COL_BLOCKS) for r0 in range(0, rows, CONV_ROW_BLOCK)]
    assert sum(GROUP_PIECES) == len(pieces)
    bounds = [sum(GROUP_PIECES[:g]) for g in range(FRONT_GROUPS + 1)]

    def emit_group(g):
        bits = None
        for j, r0 in pieces[bounds[g]:bounds[g + 1]]:
            piece = _conv_piece(cbuf, prev, wdw, j, r0, CONV_ROW_BLOCK)
            dw_scr[r0:r0 + CONV_ROW_BLOCK, j * LANES:(j + 1) * LANES] = piece
            bits = _all_bits(piece) if bits is None else bits | _all_bits(piece)
        return None if bits is None else bits & zero_ref[...]

    _, a_term, gate_b, c = _front(xc_ref[...], n_scr, gmix, win, gsgv, bsgv, wsg, bsg, wa, chunk,
                                  emit_group=emit_group)
    dw = dw_scr[...] + bdw[...]
    a_ring[cur] = a_term
    g_ring[cur] = gate_b
    first = lax.rem(t, tiles_per_seq) == 0
    for j in range(COL_BLOCKS):
        lanes = slice(j * LANES, (j + 1) * LANES)
        tail = cbuf[prev * COL_BLOCKS + j, rows:rows + HIST_PAD, :]
        cbuf[cur * COL_BLOCKS + j, 0:HIST_PAD, :] = jnp.where(first, 0.0, tail)
        cbuf[cur * COL_BLOCKS + j, HIST_PAD:HIST_PAD + rows, :] = c[:, lanes]

    for j in range(COL_BLOCKS):
        lanes = slice(j * LANES, (j + 1) * LANES)
        cs_ref[:, lanes] = cbuf[prev * COL_BLOCKS + j, rows + HIST_OFF:rows + HIST_PAD, :]
    h_ref[...] = _back(xp_ref[...], a_ring[prev], g_ring[prev], dw, gcln, bcln, wb, wout)

    @pl.when(t == tiles)
    def _():
        x = xs_ref[...]
        vn, a_s, g_s, c_s = _front(x, n_scr, gmix, win, gsgv, bsgv, wsg_s, bsg, wa, rows_s)
        vns_ref[...] = vn
        for q in range(seqs_s):
            for j in range(COL_BLOCKS):
                lanes = slice(j * LANES, (j + 1) * LANES)
                cbuf_s[q * COL_BLOCKS + j, HIST_OFF:HIST_PAD, :] = hist_ref[q, :, lanes]
                cbuf_s[q * COL_BLOCKS + j, HIST_PAD:HIST_PAD + rows_s, :] = (
                    c_s[q * rows_s:(q + 1) * rows_s, lanes])
        dw_s = jnp.concatenate([_depthwise_conv(cbuf_s, q, rows_s, wdw) for q in range(seqs_s)],
                               axis=0)
        for q in range(seqs_s):
            for j in range(COL_BLOCKS):
                lanes = slice(j * LANES, (j + 1) * LANES)
                css_ref[q, :, lanes] = cbuf_s[q * COL_BLOCKS + j,
                                              rows_s + HIST_OFF:rows_s + HIST_PAD, :]
        hs_ref[...] = _back(x, a_s, g_s, dw_s + bdw[...], gcln, bcln, wb, wout)


def _ffn_rows(h, p, gffn, wup, wdown, gple, wgate, wple, gfinal):
    f = _dot(_rms(h, gffn[...]).astype(BF16), wup[...])
    r = jnp.square(jnp.maximum(f, 0.0)).astype(BF16)
    h = h + _dot(r, wdown[...])
    gate = jax.nn.sigmoid(_dot(_rms(h, gple[...]).astype(BF16), wgate[...]))
    e = _dot(p.astype(BF16), wple[...])
    h = h + gate * e
    return _rms(h, gfinal[...])


def _ffn_kernel(h_ref, p_ref, hs_ref, ps_ref, gffn, wup_hbm, wdown_hbm, gple, wgate_hbm, wple_hbm,
                gfinal, y_ref, ys_ref, wup, wdown, wgate, wple, stage, sem):
    @pl.when(pl.program_id(0) == 0)
    def _():
        _load_weights_bf16([(wup_hbm, wup), (wdown_hbm, wdown), (wgate_hbm, wgate),
                            (wple_hbm, wple)], stage, sem)

    y_ref[...] = _ffn_rows(h_ref[...], p_ref[...], gffn, wup, wdown, gple, wgate, wple, gfinal)

    @pl.when(pl.program_id(0) == pl.num_programs(0) - 1)
    def _():
        ys_ref[...] = _ffn_rows(hs_ref[...], ps_ref[...], gffn, wup, wdown, gple, wgate, wple,
                                gfinal)


def _resident(shape):
    zeros = (0,) * len(shape)
    return pl.BlockSpec(shape, lambda *_: zeros, pipeline_mode=pl.Buffered(1))


def _full(shape):
    zeros = (0,) * len(shape)
    return pl.BlockSpec(shape, lambda *_: zeros)


def _weight_specs(weights):
    return [pl.BlockSpec(memory_space=pl.ANY) if _is_matrix(w) else _resident(w.shape)
            for w in weights]


def _is_matrix(w):
    return w.ndim == 2 and w.dtype == F32 and w.shape[0] >= PLE_DIM and w.shape[1] >= STAGE_COLS


def _weight_scratch(weights):
    return ([pltpu.VMEM(w.shape, BF16) for w in weights if _is_matrix(w)]
            + [pltpu.VMEM((2, STAGE_ROWS, STAGE_COLS), F32), pltpu.SemaphoreType.DMA((2,))])


def _params(semantics):
    return pltpu.CompilerParams(dimension_semantics=semantics, vmem_limit_bytes=VMEM_LIMIT_BYTES)


def _mixer(x, xs, hist, wsg_s, mw, chunk):
    batch, seq, d = x.shape
    seqs_s, rows_s, _ = xs.shape
    rows = MIXER_ROWS
    total_s = seqs_s * rows_s
    assert seq % rows == 0 and rows % chunk == 0 and total_s == rows
    tiles_per_seq = seq // rows
    tiles = batch * tiles_per_seq
    kern = functools.partial(_mixer_kernel, rows=rows, chunk=chunk, tiles_per_seq=tiles_per_seq,
                             tiles=tiles, seqs_s=seqs_s, rows_s=rows_s)
    cur_tile = lambda t: (jnp.minimum(t, tiles - 1), 0)
    prev_tile = lambda t: (jnp.maximum(t - 1, 0), 0)
    x2 = x.reshape(tiles * rows, d)
    h, cs, hs, css, vns = pl.pallas_call(
        kern,
        grid=(tiles + 1,),
        in_specs=[pl.BlockSpec((rows, d), cur_tile), pl.BlockSpec((rows, d), prev_tile),
                  _resident((SUBLANES, LANES)), _resident((total_s, d)), _resident(hist.shape),
                  _resident(wsg_s.shape)]
        + _weight_specs(mw),
        out_specs=[pl.BlockSpec((rows, d), prev_tile),
                   pl.BlockSpec((None, HIST, d),
                                lambda t: (jnp.maximum(t - 1, 0) // tiles_per_seq, 0, 0)),
                   _full((total_s, d)), _full((seqs_s, HIST, d)), _full((total_s, d))],
        out_shape=[jax.ShapeDtypeStruct((tiles * rows, d), F32),
                   jax.ShapeDtypeStruct((batch, HIST, d), F32),
                   jax.ShapeDtypeStruct((total_s, d), F32),
                   jax.ShapeDtypeStruct((seqs_s, HIST, d), F32),
                   jax.ShapeDtypeStruct((total_s, d), F32)],
        scratch_shapes=_weight_scratch(mw)
        + [pltpu.VMEM((rows, d), BF16),
           pltpu.VMEM((rows, d), F32),
           pltpu.VMEM((2 * COL_BLOCKS, HIST_PAD + rows, LANES), F32),
           pltpu.VMEM((2, rows, d), F32),
           pltpu.VMEM((2, rows, d), F32),
           pltpu.VMEM((seqs_s * COL_BLOCKS, HIST_PAD + rows_s, LANES), F32)],
        compiler_params=_params(("arbitrary",)),
        name="mixer",
    )(x2, x2, jnp.zeros((SUBLANES, LANES), jnp.uint32), xs.reshape(total_s, d), hist, wsg_s, *mw)
    return (h.reshape(batch, seq, d), cs, hs.reshape(seqs_s, rows_s, d), css,
            vns.reshape(seqs_s, rows_s, d))


def _ffn(h, p, hs, ps, fw):
    batch, seq, d = h.shape
    total = batch * seq
    total_s = hs.shape[0] * hs.shape[1]
    rows = FFN_ROWS
    assert total % rows == 0
    y, ys = pl.pallas_call(
        _ffn_kernel,
        grid=(total // rows,),
        in_specs=[pl.BlockSpec((rows, d), lambda i: (i, 0)),
                  pl.BlockSpec((rows, PLE_DIM), lambda i: (i, 0)),
                  _resident((total_s, d)), _resident((total_s, PLE_DIM))]
        + _weight_specs(fw),
        out_specs=[pl.BlockSpec((rows, d), lambda i: (i, 0)), _full((total_s, d))],
        out_shape=[jax.ShapeDtypeStruct((total, d), F32),
                   jax.ShapeDtypeStruct((total_s, d), F32)],
        scratch_shapes=_weight_scratch(fw),
        compiler_params=_params(("arbitrary",)),
        name="ffn",
    )(h.reshape(total, d), p.reshape(total, PLE_DIM), hs.reshape(total_s, d),
      ps.reshape(total_s, PLE_DIM), *fw)
    return y.reshape(batch, seq, d), ys.reshape(hs.shape)


def kernel(x_prompt, x_sample, cache_conv, p_prompt, p_sample, g_mix, w_in, g_sgv, b_sgv, w_sg, b_sg, w_a, w_dw, b_dw, g_cln, b_cln, w_b, w_out, g_ffn, w_up, w_down, g_ple, w_ple_gate, w_ple, g_final):
    depth = w_in.shape[0]
    assert depth == 1
    row = lambda t: t.reshape(1, -1)
    dec_seq = x_sample.shape[1]
    hp, hs = x_prompt, x_sample
    conv_p, conv_s, v_s = [], [], []
    for i in range(depth):
        bias = jnp.repeat(jnp.transpose(b_sg[i]), SG_GROUP_DIM, axis=1)
        tril = lambda n: jnp.tril(w_sg[i][:, :n, :n]).astype(BF16)
        mw = (row(g_mix[i]), w_in[i], row(g_sgv[i]), row(b_sgv[i]), tril(SG_CHUNK), bias, w_a[i],
              w_dw[i], row(b_dw[i]), row(g_cln[i]), row(b_cln[i]), w_b[i], w_out[i])
        fw = (row(g_ffn[i]), w_up[i], w_down[i], row(g_ple[i]), w_ple_gate[i], w_ple[i],
              row(g_final))

        h1p, cp, h1s, cs, vs = _mixer(hp, hs, cache_conv[i], tril(dec_seq), mw, SG_CHUNK)
        hp, hs = _ffn(h1p, p_prompt[i], h1s, p_sample[i], fw)
        conv_p.append(cp)
        conv_s.append(cs)
        v_s.append(vs)
    return (hp, hs, jnp.stack(conv_p), jnp.stack(conv_s), jnp.stack(v_s))
```
